```python
import math
import jax, jax.numpy as jnp
from jax import lax
import numpy as np

D_MODEL = 4096
BATCH = 2
SEQ = 4096
DEPTH = 1
DEC_BATCH = 4
DEC_SEQ = 4096
PAST_LEN = 128

N_META = 16
D_MIX = D_MODEL
ATTN_WIDTH = D_MIX // 2
POOL_WIDTH = D_MIX - ATTN_WIDTH
N_HEADS = 16
HEAD_DIM = ATTN_WIDTH // (2 * N_HEADS)
V_DIM = 2 * HEAD_DIM
ROT_DIM = HEAD_DIM // 4
ROPE_THETA = 500000.0
ATTN_SCALE = HEAD_DIM ** -0.5
Q_BLOCK = 128
SUBLN_EPS = 1e-5
POOL_WINDOWS = (2, 4, 8, 16)
N_POOL_GROUPS = len(POOL_WINDOWS)
POOL_CH = POOL_WIDTH // N_POOL_GROUPS
D_IN = 3 * ATTN_WIDTH + POOL_WIDTH
N_GROUPS = 4
EXPERTS_PER_GROUP = 8
N_EXPERTS = N_GROUPS * EXPERTS_PER_GROUP
TOP_K = 2
D_EXPERT = D_MODEL // 4
MOE_BLOCK = 128
NORM_EPS = 1e-6

kernel_name = 'hymba_diffattn_pool_hmoe_encoder'


def rms_norm(x, w, eps):
    x32 = x.astype(jnp.float32)
    y = x32 * lax.rsqrt(jnp.mean(x32 * x32, axis=-1, keepdims=True) + eps)
    return (y * w.astype(jnp.float32)).astype(x.dtype)


def lambda_init_fn(layer):
    return 0.8 - 0.6 * math.exp(-0.3 * layer)


def rope_tables(T):
    pos = jnp.arange(T, dtype=jnp.float32)
    inv_freq = ROPE_THETA ** (-jnp.arange(0, ROT_DIM, 2, dtype=jnp.float32) / ROT_DIM)
    ang = pos[:, None] * inv_freq[None, :]
    return jnp.cos(ang), jnp.sin(ang)


def partial_rope(x, cos, sin):
    x32 = x.astype(jnp.float32)
    half = ROT_DIM // 2
    x1 = x32[..., :half]
    x2 = x32[..., half:ROT_DIM]
    c = cos[None, :, None, None, :]
    s = sin[None, :, None, None, :]
    out = jnp.concatenate([x1 * c - x2 * s, x2 * c + x1 * s, x32[..., ROT_DIM:]], axis=-1)
    return out.astype(x.dtype)


def diff_attention(q, k, v, lq1, lk1, lq2, lk2, subln_w, lam_init):
    B, T, _ = q.shape
    dtype = q.dtype
    q = q.reshape(B, T, N_HEADS, 2, HEAD_DIM)
    k = k.reshape(B, T, N_HEADS, 2, HEAD_DIM)
    v32 = v.reshape(B, T, N_HEADS, V_DIM).astype(jnp.float32)
    cos, sin = rope_tables(T)
    q = partial_rope(q, cos, sin)
    k = partial_rope(k, cos, sin)
    lam = (jnp.exp(jnp.sum(lq1.astype(jnp.float32) * lk1.astype(jnp.float32)))
           - jnp.exp(jnp.sum(lq2.astype(jnp.float32) * lk2.astype(jnp.float32)))
           + lam_init)
    n_blk = -(-T // Q_BLOCK)
    T_pad = n_blk * Q_BLOCK
    q = jnp.pad(q, ((0, 0), (0, T_pad - T), (0, 0), (0, 0), (0, 0)))
    qb = q.reshape(B, n_blk, Q_BLOCK, N_HEADS, 2, HEAD_DIM).transpose(1, 0, 2, 3, 4, 5)

    def block(qi):
        s = jnp.einsum('bqhcd,bkhcd->bhcqk', qi, k,
                       preferred_element_type=jnp.float32) * ATTN_SCALE
        p = jax.nn.softmax(s, axis=-1)
        a = p[:, :, 0] - lam * p[:, :, 1]
        return jnp.einsum('bhqk,bkhe->bqhe', a, v32)

    o = lax.map(block, qb)
    o = o.transpose(1, 0, 2, 3, 4).reshape(B, T_pad, N_HEADS, V_DIM)[:, :T]
    o = rms_norm(o, subln_w, SUBLN_EPS) * (1.0 - lam_init)
    return o.reshape(B, T, N_HEADS * V_DIM).astype(dtype)


def pool_mixer(p, pool_w, pool_scale):
    B, T, _ = p.shape
    p32 = p.astype(jnp.float32)
    c = jnp.concatenate([jnp.zeros((B, 1, POOL_WIDTH), jnp.float32),
                         jnp.cumsum(p32, axis=1)], axis=1)
    t = np.arange(T)
    outs = []
    for g, w in enumerate(POOL_WINDOWS):
        sl = slice(g * POOL_CH, (g + 1) * POOL_CH)
        lo = np.clip(t - w // 2, 0, T - 1)
        hi = np.clip(t + w // 2 - 1, 0, T - 1)
        cnt = jnp.asarray((hi - lo + 1).astype(np.float32))
        cg = c[..., sl]
        mean = (jnp.take(cg, jnp.asarray(hi + 1), axis=1)
                - jnp.take(cg, jnp.asarray(lo), axis=1)) / cnt[None, :, None]
        outs.append(mean - p32[..., sl])
    z = jnp.stack(outs, axis=2).astype(p.dtype)
    y = jnp.einsum('btgc,gcd->btgd', z, pool_w)
    return (y.reshape(B, T, POOL_WIDTH) * pool_scale).astype(p.dtype)


def hierarchical_moe(h, router_group, router_expert, w_gate, w_up, w_down):
    B, T, D = h.shape
    N = B * T
    xf = h.reshape(N, D)
    g_logits = jnp.matmul(xf, router_group).astype(jnp.float32)
    g_prob = jax.nn.softmax(g_logits, axis=-1)
    g_sel = jnp.argmax(g_logits, axis=-1).astype(jnp.int32)
    p_g = jnp.take_along_axis(g_prob, g_sel[:, None], axis=1)[:, 0]
    e_logits = jnp.matmul(xf, router_expert).astype(jnp.float32).reshape(N, N_GROUPS, EXPERTS_PER_GROUP)
    e_sel_logits = jnp.take_along_axis(e_logits, g_sel[:, None, None], axis=1)[:, 0]
    top_v, top_i = lax.top_k(e_sel_logits, TOP_K)
    top_w = jax.nn.softmax(top_v, axis=-1) * p_g[:, None]
    expert_id = (g_sel[:, None] * EXPERTS_PER_GROUP + top_i).reshape(-1).astype(jnp.int32)
    tok_id = jnp.repeat(jnp.arange(N, dtype=jnp.int32), TOP_K)
    wts = top_w.reshape(-1)
    A = N * TOP_K
    order = jnp.argsort(expert_id, stable=True)
    e_s = expert_id[order]
    t_s = tok_id[order]
    w_s = wts[order]
    counts = jnp.bincount(expert_id, length=N_EXPERTS).astype(jnp.int32)
    starts = jnp.cumsum(counts) - counts
    pcounts = ((counts + MOE_BLOCK - 1) // MOE_BLOCK) * MOE_BLOCK
    pends = jnp.cumsum(pcounts)
    pstarts = pends - pcounts
    dest = pstarts[e_s] + jnp.arange(A, dtype=jnp.int32) - starts[e_s]
    n_blocks = -(-A // MOE_BLOCK) + N_EXPERTS
    P = n_blocks * MOE_BLOCK
    tok_buf = jnp.full((P,), N, dtype=jnp.int32).at[dest].set(t_s)
    w_buf = jnp.zeros((P,), jnp.float32).at[dest].set(w_s)
    block_start = jnp.arange(n_blocks, dtype=jnp.int32) * MOE_BLOCK
    block_e = jnp.minimum(jnp.sum(block_start[:, None] >= pends[None, :], axis=1),
                          N_EXPERTS - 1).astype(jnp.int32)
    x_pad = jnp.concatenate([xf, jnp.zeros((1, D), xf.dtype)], axis=0)
    xb = x_pad[tok_buf].reshape(n_blocks, MOE_BLOCK, D)

    def expert_block(args):
        xi, e = args
        gt = jnp.matmul(xi, w_gate[e])
        up = jnp.matmul(xi, w_up[e])
        return jnp.matmul(jax.nn.silu(gt) * up, w_down[e])

    yb = lax.map(expert_block, (xb, block_e)).reshape(P, D)
    contrib = yb.astype(jnp.float32) * w_buf[:, None]
    y = jax.ops.segment_sum(contrib, tok_buf, num_segments=N + 1)[:N]
    return y.astype(h.dtype).reshape(B, T, D)


def encoder_trunk(x, meta_tokens, attn_norm_w, w_in, lambda_q1, lambda_k1, lambda_q2,
                  lambda_k2, subln_w, pool_w, pool_scale, w_out, ffn_norm_w,
                  router_group, router_expert, expert_w_gate, expert_w_up,
                  expert_w_down, final_norm_w):
    B = x.shape[0]
    meta = jnp.broadcast_to(meta_tokens.astype(x.dtype)[None], (B, N_META, D_MODEL))
    h = jnp.concatenate([meta, x], axis=1)
    for l in range(DEPTH):
        hn = rms_norm(h, attn_norm_w[l], NORM_EPS)
        proj = jnp.matmul(hn, w_in[l])
        q = proj[..., :ATTN_WIDTH]
        k = proj[..., ATTN_WIDTH:2 * ATTN_WIDTH]
        v = proj[..., 2 * ATTN_WIDTH:3 * ATTN_WIDTH]
        p = proj[..., 3 * ATTN_WIDTH:]
        attn_out = diff_attention(q, k, v, lambda_q1[l], lambda_k1[l], lambda_q2[l],
                                  lambda_k2[l], subln_w[l], lambda_init_fn(l))
        pool_out = pool_mixer(p, pool_w[l], pool_scale[l])
        mix = jnp.concatenate([attn_out, pool_out], axis=-1)
        h = h + jnp.matmul(mix, w_out[l])
        hn = rms_norm(h, ffn_norm_w[l], NORM_EPS)
        h = h + hierarchical_moe(hn, router_group[l], router_expert[l], expert_w_gate[l],
                                 expert_w_up[l], expert_w_down[l])
    h = rms_norm(h, final_norm_w, NORM_EPS)
    return h[:, N_META:]


def setup_inputs(seed: int = 0) -> dict:
    key = jax.random.key(seed)
    ks = jax.random.split(key, 20)

    def nrm(k, shape, scale):
        return jax.random.normal(k, shape, jnp.float32) * scale

    return {
        'x_prompt': nrm(ks[0], (BATCH, SEQ, D_MODEL), 1.0),
        'x_sample': nrm(ks[1], (DEC_BATCH, DEC_SEQ, D_MODEL), 1.0),
        'meta_tokens': nrm(ks[2], (N_META, D_MODEL), 1.0),
        'attn_norm_w': 1.0 + nrm(ks[3], (DEPTH, D_MODEL), 0.05),
        'w_in': nrm(ks[4], (DEPTH, D_MODEL, D_IN), D_MODEL ** -0.5),
        'lambda_q1': nrm(ks[5], (DEPTH, HEAD_DIM), 0.1),
        'lambda_k1': nrm(ks[6], (DEPTH, HEAD_DIM), 0.1),
        'lambda_q2': nrm(ks[7], (DEPTH, HEAD_DIM), 0.1),
        'lambda_k2': nrm(ks[8], (DEPTH, HEAD_DIM), 0.1),
        'subln_w': 1.0 + nrm(ks[9], (DEPTH, V_DIM), 0.05),
        'pool_w': nrm(ks[10], (DEPTH, N_POOL_GROUPS, POOL_CH, POOL_CH), POOL_CH ** -0.5),
        'pool_scale': 1.0 + nrm(ks[11], (DEPTH, POOL_WIDTH), 0.1),
        'w_out': nrm(ks[12], (DEPTH, D_MIX, D_MODEL), D_MIX ** -0.5),
        'ffn_norm_w': 1.0 + nrm(ks[13], (DEPTH, D_MODEL), 0.05),
        'router_group': nrm(ks[14], (DEPTH, D_MODEL, N_GROUPS), D_MODEL ** -0.5),
        'router_expert': nrm(ks[15], (DEPTH, D_MODEL, N_EXPERTS), D_MODEL ** -0.5),
        'expert_w_gate': nrm(ks[16], (DEPTH, N_EXPERTS, D_MODEL, D_EXPERT), D_MODEL ** -0.5),
        'expert_w_up': nrm(ks[17], (DEPTH, N_EXPERTS, D_MODEL, D_EXPERT), D_MODEL ** -0.5),
        'expert_w_down': nrm(ks[18], (DEPTH, N_EXPERTS, D_EXPERT, D_MODEL), D_EXPERT ** -0.5),
        'final_norm_w': 1.0 + nrm(ks[19], (D_MODEL,), 0.05),
    }


def reference(x_prompt, x_sample, meta_tokens, attn_norm_w, w_in, lambda_q1, lambda_k1,
              lambda_q2, lambda_k2, subln_w, pool_w, pool_scale, w_out, ffn_norm_w,
              router_group, router_expert, expert_w_gate, expert_w_up, expert_w_down,
              final_norm_w):
    y_prompt = encoder_trunk(x_prompt, meta_tokens, attn_norm_w, w_in, lambda_q1, lambda_k1,
                             lambda_q2, lambda_k2, subln_w, pool_w, pool_scale, w_out,
                             ffn_norm_w, router_group, router_expert, expert_w_gate,
                             expert_w_up, expert_w_down, final_norm_w)
    y_sample = encoder_trunk(x_sample, meta_tokens, attn_norm_w, w_in, lambda_q1, lambda_k1,
                             lambda_q2, lambda_k2, subln_w, pool_w, pool_scale, w_out,
                             ffn_norm_w, router_group, router_expert, expert_w_gate,
                             expert_w_up, expert_w_down, final_norm_w)
    return (y_prompt, y_sample)
```

```python
import functools
import math

import jax
import jax.numpy as jnp
from jax import lax
from jax.experimental import pallas as pl
from jax.experimental.pallas import tpu as pltpu

F32 = jnp.float32
BF16 = jnp.bfloat16

N_META = 16
HEAD_DIM = 64
V_DIM = 2 * HEAD_DIM
ROT_DIM = HEAD_DIM // 4
ROPE_THETA = 500000.0
ATTN_SCALE = HEAD_DIM ** -0.5
SUBLN_EPS = 1e-5
NORM_EPS = 1e-6
POOL_WINDOWS = (2, 4, 8, 16)
N_GROUPS = 4
EXPERTS_PER_GROUP = 8
N_EXPERTS = N_GROUPS * EXPERTS_PER_GROUP
TOP_K = 2
LAMBDA_INIT = 0.8 - 0.6 * math.exp(-0.3 * 0)

LANES = 128
SUBLANES = 8
META_ROWS = 128
HALO = 8
MASK_VALUE = -1e30

VMEM_LIMIT = 56 * 1024 * 1024


def _cparams(sem):
    return pltpu.CompilerParams(dimension_semantics=sem, vmem_limit_bytes=VMEM_LIMIT)


def _inproj_kernel(x_ref, nw_ref, w_ref, c_ref, a_ref, b_ref,
                   qk_ref, v_ref, p_ref, hn_ref, *, nqk, nv):
    j = pl.program_id(1)

    @pl.when(j == 0)
    def _():
        x = x_ref[...]
        ms = jnp.mean(x * x, axis=-1, keepdims=True)
        hn_ref[...] = (x * lax.rsqrt(ms + NORM_EPS) * nw_ref[...]).astype(BF16)

    acc = jnp.dot(hn_ref[...], w_ref[...], preferred_element_type=F32)

    @pl.when(j < nqk)
    def _():
        scale = jnp.where(j < nqk // 2, ATTN_SCALE, 1.0).astype(F32)
        c = c_ref[...]
        a = a_ref[...]
        b = b_ref[...]
        for t in range(acc.shape[1] // LANES):
            blk = acc[:, t * LANES:(t + 1) * LANES]
            up = pltpu.roll(blk, LANES - ROT_DIM // 2, 1)
            dn = pltpu.roll(blk, ROT_DIM // 2, 1)
            r = blk * c + up * a + dn * b
            qk_ref[:, t * LANES:(t + 1) * LANES] = (r * scale).astype(BF16)

    @pl.when((j >= nqk) & (j < nqk + nv))
    def _():
        v_ref[...] = acc.astype(BF16)

    @pl.when(j >= nqk + nv)
    def _():
        p_ref[...] = acc


def _rope_tables(pos):
    half = ROT_DIM // 2
    inv_freq = ROPE_THETA ** (-jnp.arange(0, ROT_DIM, 2, dtype=F32) / ROT_DIM)
    ang = pos[:, None] * inv_freq[None, :]
    cos, sin = jnp.cos(ang), jnp.sin(ang)
    r = pos.shape[0]
    ones = jnp.ones((r, HEAD_DIM - ROT_DIM), F32)
    zeros = jnp.zeros((r, HEAD_DIM - ROT_DIM), F32)
    zh = jnp.zeros((r, half), F32)
    c = jnp.concatenate([cos, cos, ones], axis=1)
    a = jnp.concatenate([-sin, zh, zeros], axis=1)
    b = jnp.concatenate([zh, sin, zeros], axis=1)
    rep = lambda t: jnp.concatenate([t, t], axis=1)
    return rep(c), rep(a), rep(b)


def _inproj(x2d, norm_w, w_in_bf, tables, rows_per_seq, tm, tn):
    n, d = x2d.shape
    d_in = w_in_bf.shape[1]
    attn_w = d // 2
    nqk = 2 * attn_w // tn
    nv = attn_w // tn
    npool = (d_in - 3 * attn_w) // tn
    tps = rows_per_seq // tm
    c, a, b = tables
    tab_spec = pl.BlockSpec((tm, LANES), lambda i, j: (i % tps, 0))
    kern = functools.partial(_inproj_kernel, nqk=nqk, nv=nv)
    return pl.pallas_call(
        kern,
        grid=(n // tm, d_in // tn),
        in_specs=[
            pl.BlockSpec((tm, d), lambda i, j: (i, 0)),
            pl.BlockSpec((1, d), lambda i, j: (0, 0)),
            pl.BlockSpec((d, tn), lambda i, j: (0, j)),
            tab_spec, tab_spec, tab_spec,
        ],
        out_specs=[
            pl.BlockSpec((tm, tn), lambda i, j: (i, jnp.minimum(j, nqk - 1))),
            pl.BlockSpec((tm, tn), lambda i, j: (i, jnp.clip(j - nqk, 0, nv - 1))),
            pl.BlockSpec((tm, tn), lambda i, j: (i, jnp.clip(j - nqk - nv, 0, npool - 1))),
        ],
        out_shape=[
            jax.ShapeDtypeStruct((n, 2 * attn_w), BF16),
            jax.ShapeDtypeStruct((n, attn_w), BF16),
            jax.ShapeDtypeStruct((n, d_in - 3 * attn_w), F32),
        ],
        scratch_shapes=[pltpu.VMEM((tm, d), BF16)],
        compiler_params=_cparams(("arbitrary", "arbitrary")),
        name="inproj",
    )(x2d, norm_w.reshape(1, d), w_in_bf, c, a, b)


def _lam_kernel(q1_ref, k1_ref, q2_ref, k2_ref, o_ref):
    s1 = jnp.sum(q1_ref[...] * k1_ref[...], axis=-1, keepdims=True)
    s2 = jnp.sum(q2_ref[...] * k2_ref[...], axis=-1, keepdims=True)
    o_ref[...] = jnp.exp(s1) - jnp.exp(s2) + LAMBDA_INIT


def _lam(lq1, lk1, lq2, lk2):
    r = lambda t: t.reshape(1, -1).astype(F32)
    return pl.pallas_call(
        _lam_kernel,
        out_shape=jax.ShapeDtypeStruct((1, 1), F32),
        name="lam",
    )(r(lq1), r(lk1), r(lq2), r(lk2))


def _attn_kernel(lam_ref, q_ref, k_ref, km_ref, v_ref, vm_ref, sw_ref, o_ref):
    tq = q_ref.shape[0]
    q = q_ref[...]
    lane = lax.broadcasted_iota(jnp.int32, q.shape, 1)
    zero = jnp.zeros_like(q)
    qs = jnp.concatenate([jnp.where(lane < HEAD_DIM, q, zero),
                          jnp.where(lane >= HEAD_DIM, q, zero)], axis=0)
    dn = (((1,), (1,)), ((), ()))
    s = lax.dot_general(qs, k_ref[...], dn, preferred_element_type=F32)
    sm = lax.dot_general(qs, km_ref[...], dn, preferred_element_type=F32)
    col = lax.broadcasted_iota(jnp.int32, sm.shape, 1)
    sm = jnp.where(col < N_META, sm, MASK_VALUE)
    m = jnp.maximum(jnp.max(s, axis=-1, keepdims=True), jnp.max(sm, axis=-1, keepdims=True))
    e = jnp.exp(s - m)
    em = jnp.exp(sm - m)
    l = jnp.sum(e, axis=-1, keepdims=True) + jnp.sum(em, axis=-1, keepdims=True)
    r = 1.0 / l
    r0 = r[:tq]
    r1 = r[tq:] * lam_ref[...]
    a = (e[:tq] * r0 - e[tq:] * r1).astype(BF16)
    am = (em[:tq] * r0 - em[tq:] * r1).astype(BF16)
    o = (jnp.dot(a, v_ref[...], preferred_element_type=F32)
         + jnp.dot(am, vm_ref[...], preferred_element_type=F32))
    ms = jnp.mean(o * o, axis=-1, keepdims=True)
    y = o * lax.rsqrt(ms + SUBLN_EPS) * sw_ref[...]
    o_ref[...] = (y * (1.0 - LAMBDA_INIT)).astype(BF16)


def _attention(lam, qk, v, qk_meta, v_meta, subln_w, batch, seq, tq):
    n, two_w = qk.shape
    n_heads = two_w // 2 // V_DIM
    nqt = seq // tq
    return pl.pallas_call(
        _attn_kernel,
        grid=(batch, n_heads, nqt),
        in_specs=[
            pl.BlockSpec((1, 1), lambda b, h, i: (0, 0)),
            pl.BlockSpec((tq, V_DIM), lambda b, h, i: (b * nqt + i, h)),
            pl.BlockSpec((seq, V_DIM), lambda b, h, i: (b, n_heads + h)),
            pl.BlockSpec((META_ROWS, V_DIM), lambda b, h, i: (0, n_heads + h)),
            pl.BlockSpec((seq, V_DIM), lambda b, h, i: (b, h)),
            pl.BlockSpec((META_ROWS, V_DIM), lambda b, h, i: (0, h)),
            pl.BlockSpec((1, V_DIM), lambda b, h, i: (0, 0)),
        ],
        out_specs=pl.BlockSpec((tq, V_DIM), lambda b, h, i: (b * nqt + i, h)),
        out_shape=jax.ShapeDtypeStruct((n, two_w // 2), BF16),
        compiler_params=_cparams(("arbitrary", "arbitrary", "arbitrary")),
        name="attn",
    )(lam, qk, qk, qk_meta, v, v_meta, subln_w.reshape(1, V_DIM).astype(F32))


def _pool_kernel(p_ref, prev_ref, next_ref, pm_ref, w_ref, sc_ref, o_ref, *, tps, total_t):
    i = pl.program_id(0)
    tm = p_ref.shape[0]
    ti = i % tps
    p = p_ref[...]
    prev = jnp.where(ti == 0, pm_ref[...], prev_ref[...])
    nxt = jnp.where(ti == tps - 1, 0.0, next_ref[...])
    ext = jnp.concatenate([prev, p, nxt], axis=0)
    rows = tm + 2 * HALO
    shift = lambda t, k: pltpu.roll(t, rows - k, 0)
    t_pos = N_META + ti * tm + lax.broadcasted_iota(jnp.int32, (tm, 1), 0)
    pch = p.shape[1] // len(POOL_WINDOWS)
    for g, w in enumerate(POOL_WINDOWS):
        sl = slice(g * pch, (g + 1) * pch)
        acc = ext[:, sl]
        span = 1
        while span < w:
            acc = acc + shift(acc, span)
            span *= 2
        off = HALO - w // 2
        win = (shift(acc, off) if off else acc)[:tm]
        cnt = jnp.minimum(w, total_t - t_pos + w // 2).astype(F32)
        z = (win / cnt - p[:, sl]).astype(BF16)
        y = jnp.dot(z, w_ref[g], preferred_element_type=F32)
        o_ref[:, sl] = (y * sc_ref[:, sl]).astype(BF16)


def _pool(p, p_meta, pool_w_bf, pool_scale, seq, tm):
    n, pw = p.shape
    tps = seq // tm
    hb = tm // HALO
    nb = n // HALO
    kern = functools.partial(_pool_kernel, tps=tps, total_t=seq + N_META)
    return pl.pallas_call(
        kern,
        grid=(n // tm,),
        in_specs=[
            pl.BlockSpec((tm, pw), lambda i: (i, 0)),
            pl.BlockSpec((HALO, pw), lambda i: (jnp.maximum(i * hb - 1, 0), 0)),
            pl.BlockSpec((HALO, pw), lambda i: (jnp.minimum((i + 1) * hb, nb - 1), 0)),
            pl.BlockSpec((HALO, pw), lambda i: (N_META // HALO - 1, 0)),
            pl.BlockSpec(pool_w_bf.shape, lambda i: (0, 0, 0)),
            pl.BlockSpec((1, pw), lambda i: (0, 0)),
        ],
        out_specs=pl.BlockSpec((tm, pw), lambda i: (i, 0)),
        out_shape=jax.ShapeDtypeStruct((n, pw), BF16),
        compiler_params=_cparams(("arbitrary",)),
        name="pool",
    )(p, p, p, p_meta, pool_w_bf, pool_scale.reshape(1, pw).astype(F32))


def _outproj_kernel(a_ref, p_ref, wa_ref, wp_ref, x_ref, o_ref):
    acc = jnp.dot(a_ref[...], wa_ref[...], preferred_element_type=F32)
    acc = acc + jnp.dot(p_ref[...], wp_ref[...], preferred_element_type=F32)
    o_ref[...] = x_ref[...] + acc


def _outproj(attn, pool, w_out_bf, x2d, tm, tn):
    n, d = x2d.shape
    half = attn.shape[1]
    return pl.pallas_call(
        _outproj_kernel,
        grid=(n // tm, d // tn),
        in_specs=[
            pl.BlockSpec((tm, half), lambda i, j: (i, 0)),
            pl.BlockSpec((tm, half), lambda i, j: (i, 0)),
            pl.BlockSpec((half, tn), lambda i, j: (0, j)),
            pl.BlockSpec((half, tn), lambda i, j: (1, j)),
            pl.BlockSpec((tm, tn), lambda i, j: (i, j)),
        ],
        out_specs=pl.BlockSpec((tm, tn), lambda i, j: (i, j)),
        out_shape=jax.ShapeDtypeStruct((n, d), F32),
        compiler_params=_cparams(("arbitrary", "arbitrary")),
        name="outproj",
    )(attn, pool, w_out_bf, w_out_bf, x2d)


def _router_kernel(h_ref, nw_ref, wr_ref, hn_ref, lg_ref):
    h = h_ref[...]
    ms = jnp.mean(h * h, axis=-1, keepdims=True)
    hn = h * lax.rsqrt(ms + NORM_EPS) * nw_ref[...]
    hn_ref[...] = hn
    lg_ref[...] = jnp.dot(hn.astype(BF16), wr_ref[...], preferred_element_type=F32)


def _router(h1, norm_w, wr_bf, tm):
    n, d = h1.shape
    return pl.pallas_call(
        _router_kernel,
        grid=(n // tm,),
        in_specs=[
            pl.BlockSpec((tm, d), lambda i: (i, 0)),
            pl.BlockSpec((1, d), lambda i: (0, 0)),
            pl.BlockSpec((d, LANES), lambda i: (0, 0)),
        ],
        out_specs=[
            pl.BlockSpec((tm, d), lambda i: (i, 0)),
            pl.BlockSpec((tm, LANES), lambda i: (i, 0)),
        ],
        out_shape=[
            jax.ShapeDtypeStruct((n, d), F32),
            jax.ShapeDtypeStruct((n, LANES), F32),
        ],
        compiler_params=_cparams(("arbitrary",)),
        name="router",
    )(h1, norm_w.reshape(1, d), wr_bf)


def _route(logits, bm):
    n = logits.shape[0]
    gl = logits[:, :N_GROUPS]
    el = logits[:, N_GROUPS:N_GROUPS + N_EXPERTS].reshape(n, N_GROUPS, EXPERTS_PER_GROUP)
    g_prob = jax.nn.softmax(gl, axis=-1)
    g_sel = jnp.argmax(gl, axis=-1).astype(jnp.int32)
    p_g = jnp.take_along_axis(g_prob, g_sel[:, None], axis=1)[:, 0]
    esl = jnp.take_along_axis(el, g_sel[:, None, None], axis=1)[:, 0]
    top_v, top_i = lax.top_k(esl, TOP_K)
    top_w = jax.nn.softmax(top_v, axis=-1) * p_g[:, None]
    eid = (g_sel[:, None] * EXPERTS_PER_GROUP + top_i).reshape(-1).astype(jnp.int32)
    wts = top_w.reshape(-1)
    na = n * TOP_K
    onehot = (eid[:, None] == jnp.arange(N_EXPERTS, dtype=jnp.int32)[None, :]).astype(jnp.int32)
    csum = jnp.cumsum(onehot, axis=0)
    rank = jnp.take_along_axis(csum, eid[:, None], axis=1)[:, 0] - 1
    counts = csum[-1]
    pcounts = ((counts + bm - 1) // bm) * bm
    pends = jnp.cumsum(pcounts)
    pstarts = pends - pcounts
    dest = pstarts[eid] + rank
    n_blocks = -(-na // bm) + N_EXPERTS
    rows = n_blocks * bm
    aidx = jnp.arange(na, dtype=jnp.int32)
    tok = jnp.zeros((rows,), jnp.int32).at[dest].set(aidx // TOP_K)
    slot = jnp.zeros((rows,), jnp.int32).at[dest].set((aidx % TOP_K) * n + aidx // TOP_K)
    wrow = jnp.zeros((rows,), F32).at[dest].set(wts)
    bstart = jnp.arange(n_blocks, dtype=jnp.int32) * bm
    block_e = jnp.minimum(jnp.sum(bstart[:, None] >= pends[None, :], axis=1),
                          N_EXPERTS - 1).astype(jnp.int32)
    nvalid = jnp.clip(pstarts[block_e] + counts[block_e] - bstart, 0, bm).astype(jnp.int32)
    return block_e, nvalid, tok, slot, wrow.reshape(rows, 1), n_blocks


def _gateup_kernel(be_ref, nv_ref, tok_ref, x_hbm, wg_ref, wu_ref, h_ref, xbuf, sem):
    i = pl.program_id(0)
    bm = xbuf.shape[0]
    nv = nv_ref[i]

    def row_copy(r):
        t = tok_ref[i * bm + r]
        return pltpu.make_async_copy(x_hbm.at[pl.ds(t, 1)], xbuf.at[pl.ds(r, 1)], sem)

    @pl.when(i == 0)
    def _():
        xbuf[...] = jnp.zeros_like(xbuf)

    @pl.when(nv > 0)
    def _():
        def start(r, c):
            row_copy(r).start()
            return c

        def wait(r, c):
            row_copy(r).wait()
            return c

        lax.fori_loop(0, nv, start, 0)
        lax.fori_loop(0, nv, wait, 0)
        x = xbuf[...].astype(BF16)
        g = jnp.dot(x, wg_ref[0], preferred_element_type=F32)
        u = jnp.dot(x, wu_ref[0], preferred_element_type=F32)
        h_ref[...] = (g * (1.0 / (1.0 + jnp.exp(-g))) * u).astype(BF16)

    @pl.when(nv == 0)
    def _():
        h_ref[...] = jnp.zeros_like(h_ref)


def _gateup(block_e, nvalid, tok, hn2, wg_bf, wu_bf, n_blocks, bm):
    n, d = hn2.shape
    f = wg_bf.shape[2]
    grid_spec = pltpu.PrefetchScalarGridSpec(
        num_scalar_prefetch=3,
        grid=(n_blocks,),
        in_specs=[
            pl.BlockSpec(memory_space=pl.ANY),
            pl.BlockSpec((1, d, f), lambda i, be, nv, tk: (be[i], 0, 0)),
            pl.BlockSpec((1, d, f), lambda i, be, nv, tk: (be[i], 0, 0)),
        ],
        out_specs=pl.BlockSpec((bm, f), lambda i, be, nv, tk: (i, 0)),
        scratch_shapes=[pltpu.VMEM((bm, d), F32), pltpu.SemaphoreType.DMA(())],
    )
    return pl.pallas_call(
        _gateup_kernel,
        grid_spec=grid_spec,
        out_shape=jax.ShapeDtypeStruct((n_blocks * bm, f), BF16),
        compiler_params=_cparams(("arbitrary",)),
        name="gateup",
    )(block_e, nvalid, tok, hn2, wg_bf, wu_bf)


def _down_kernel(be_ref, nv_ref, slot_ref, h_ref, wd_ref, w_ref, y_hbm, ybuf, sem):
    i = pl.program_id(0)
    bm = ybuf.shape[0]
    nv = nv_ref[i]

    def row_copy(r):
        s = slot_ref[i * bm + r]
        return pltpu.make_async_copy(ybuf.at[pl.ds(r, 1)], y_hbm.at[pl.ds(s, 1)], sem)

    @pl.when(nv > 0)
    def _():
        y = jnp.dot(h_ref[...], wd_ref[0], preferred_element_type=F32)
        ybuf[...] = y * w_ref[...]

        def start(r, c):
            row_copy(r).start()
            return c

        def wait(r, c):
            row_copy(r).wait()
            return c

        lax.fori_loop(0, nv, start, 0)
        lax.fori_loop(0, nv, wait, 0)


def _down(block_e, nvalid, slot, h, wd_bf, wrow, n_assign, n_blocks, bm):
    f, d = wd_bf.shape[1], wd_bf.shape[2]
    grid_spec = pltpu.PrefetchScalarGridSpec(
        num_scalar_prefetch=3,
        grid=(n_blocks,),
        in_specs=[
            pl.BlockSpec((bm, f), lambda i, be, nv, sl: (i, 0)),
            pl.BlockSpec((1, f, d), lambda i, be, nv, sl: (be[i], 0, 0)),
            pl.BlockSpec((bm, 1), lambda i, be, nv, sl: (i, 0)),
        ],
        out_specs=pl.BlockSpec(memory_space=pl.ANY),
        scratch_shapes=[pltpu.VMEM((bm, d), F32), pltpu.SemaphoreType.DMA(())],
    )
    return pl.pallas_call(
        _down_kernel,
        grid_spec=grid_spec,
        out_shape=jax.ShapeDtypeStruct((n_assign, d), F32),
        compiler_params=_cparams(("arbitrary",)),
        name="down",
    )(block_e, nvalid, slot, h, wd_bf, wrow)


def _final_kernel(h_ref, y0_ref, y1_ref, nw_ref, o_ref):
    h = h_ref[...] + (y0_ref[...] + y1_ref[...])
    ms = jnp.mean(h * h, axis=-1, keepdims=True)
    o_ref[...] = h * lax.rsqrt(ms + NORM_EPS) * nw_ref[...]


def _final(h1, yc, norm_w, tm):
    n, d = h1.shape
    nt = n // tm
    return pl.pallas_call(
        _final_kernel,
        grid=(nt,),
        in_specs=[
            pl.BlockSpec((tm, d), lambda i: (i, 0)),
            pl.BlockSpec((tm, d), lambda i: (i, 0)),
            pl.BlockSpec((tm, d), lambda i: (i + nt, 0)),
            pl.BlockSpec((1, d), lambda i: (0, 0)),
        ],
        out_specs=pl.BlockSpec((tm, d), lambda i: (i, 0)),
        out_shape=jax.ShapeDtypeStruct((n, d), F32),
        compiler_params=_cparams(("arbitrary",)),
        name="final",
    )(h1, yc, yc, norm_w.reshape(1, d))


def _tiles(seq, d):
    tm = min(512, seq)
    return dict(tm=tm, tn=min(512, d // 2), tq=min(256, seq), tm_out=tm,
                tn_out=min(1024, d), tm_fin=min(256, seq), bm=min(256, seq))


def _trunk(x, shared, prm):
    b, seq, d = x.shape
    n = b * seq
    t = _tiles(seq, d)
    x2d = x.reshape(n, d)
    qk, v, p = _inproj(x2d, prm["attn_norm_w"], shared["w_in"], shared["rope_real"], seq,
                       t["tm"], t["tn"])
    attn = _attention(shared["lam"], qk, v, shared["qk_meta"], shared["v_meta"],
                      prm["subln_w"], b, seq, t["tq"])
    pool = _pool(p, shared["p_meta"], shared["pool_w"], prm["pool_scale"], seq, t["tm"])
    h1 = _outproj(attn, pool, shared["w_out"], x2d, t["tm_out"], t["tn_out"])
    hn2, logits = _router(h1, prm["ffn_norm_w"], shared["w_router"], t["tm"])
    bm = t["bm"]
    block_e, nvalid, tok, slot, wrow, n_blocks = _route(logits, bm)
    h = _gateup(block_e, nvalid, tok, hn2, shared["w_gate"], shared["w_up"], n_blocks, bm)
    yc = _down(block_e, nvalid, slot, h, shared["w_down"], wrow, n * TOP_K, n_blocks, bm)
    out = _final(h1, yc, prm["final_norm_w"], t["tm_fin"])
    return out.reshape(b, seq, d)


def kernel(x_prompt, x_sample, meta_tokens, attn_norm_w, w_in, lambda_q1, lambda_k1, lambda_q2, lambda_k2, subln_w, pool_w, pool_scale, w_out, ffn_norm_w, router_group, router_expert, expert_w_gate, expert_w_up, expert_w_down, final_norm_w):
    d = x_prompt.shape[-1]
    seq = x_prompt.shape[1]
    assert x_sample.shape[1] == seq and attn_norm_w.shape[0] == 1
    prm = dict(attn_norm_w=attn_norm_w[0], subln_w=subln_w[0], pool_scale=pool_scale[0],
               ffn_norm_w=ffn_norm_w[0], final_norm_w=final_norm_w)
    wr = jnp.concatenate([router_group[0], router_expert[0]], axis=1)
    wr = jnp.pad(wr, ((0, 0), (0, LANES - wr.shape[1])))
    shared = dict(
        w_in=w_in[0].astype(BF16), w_out=w_out[0].astype(BF16), pool_w=pool_w[0].astype(BF16),
        w_router=wr.astype(BF16), w_gate=expert_w_gate[0].astype(BF16),
        w_up=expert_w_up[0].astype(BF16), w_down=expert_w_down[0].astype(BF16),
        lam=_lam(lambda_q1[0], lambda_k1[0], lambda_q2[0], lambda_k2[0]),
        rope_real=_rope_tables(jnp.arange(N_META, N_META + seq, dtype=F32)),
    )
    meta_pad = jnp.pad(meta_tokens.astype(F32), ((0, META_ROWS - N_META), (0, 0)))
    t = _tiles(seq, d)
    qk_m, v_m, p_m = _inproj(meta_pad, prm["attn_norm_w"], shared["w_in"],
                             _rope_tables(jnp.arange(META_ROWS, dtype=F32)), META_ROWS,
                             META_ROWS, t["tn"])
    shared.update(qk_meta=qk_m, v_meta=v_m, p_meta=p_m)
    return (_trunk(x_prompt, shared, prm), _trunk(x_sample, shared, prm))
```

```python
import functools
import math

import jax
import jax.numpy as jnp
from jax import lax
from jax.experimental import pallas as pl
from jax.experimental.pallas import tpu as pltpu

F32 = jnp.float32
BF16 = jnp.bfloat16

N_META = 16
HEAD_DIM = 64
V_DIM = 2 * HEAD_DIM
ROT_DIM = HEAD_DIM // 4
ROPE_THETA = 500000.0
ATTN_SCALE = HEAD_DIM ** -0.5
SUBLN_EPS = 1e-5
NORM_EPS = 1e-6
POOL_WINDOWS = (2, 4, 8, 16)
N_GROUPS = 4
EXPERTS_PER_GROUP = 8
N_EXPERTS = N_GROUPS * EXPERTS_PER_GROUP
TOP_K = 2
LAMBDA_INIT = 0.8 - 0.6 * math.exp(-0.3 * 0)

LANES = 128
SUBLANES = 8
META_ROWS = 128
HALO = 8
MASK_VALUE = -1e30

VMEM_LIMIT = 56 * 1024 * 1024


def _cparams(sem):
    return pltpu.CompilerParams(dimension_semantics=sem, vmem_limit_bytes=VMEM_LIMIT)


def _inproj_kernel(x_ref, nw_ref, w_ref, c_ref, a_ref, b_ref,
                   qk_ref, v_ref, p_ref, hn_ref, *, nqk, nv):
    j = pl.program_id(1)

    @pl.when(j == 0)
    def _():
        x = x_ref[...]
        ms = jnp.mean(x * x, axis=-1, keepdims=True)
        hn_ref[...] = (x * lax.rsqrt(ms + NORM_EPS) * nw_ref[...]).astype(BF16)

    acc = jnp.dot(hn_ref[...], w_ref[...], preferred_element_type=F32)

    @pl.when(j < nqk)
    def _():
        scale = jnp.where(j < nqk // 2, ATTN_SCALE, 1.0).astype(F32)
        c = c_ref[...]
        a = a_ref[...]
        b = b_ref[...]
        for t in range(acc.shape[1] // LANES):
            blk = acc[:, t * LANES:(t + 1) * LANES]
            up = pltpu.roll(blk, LANES - ROT_DIM // 2, 1)
            dn = pltpu.roll(blk, ROT_DIM // 2, 1)
            r = blk * c + up * a + dn * b
            qk_ref[:, t * LANES:(t + 1) * LANES] = (r * scale).astype(BF16)

    @pl.when((j >= nqk) & (j < nqk + nv))
    def _():
        lane = lax.broadcasted_iota(jnp.int32, (acc.shape[0], LANES), 1)
        ones_col = jnp.where(lane == 0, 1.0, 0.0).astype(BF16)
        for t in range(acc.shape[1] // LANES):
            v_ref[:, 2 * t * LANES:(2 * t + 1) * LANES] = acc[:, t * LANES:(t + 1) * LANES].astype(BF16)
            v_ref[:, (2 * t + 1) * LANES:(2 * t + 2) * LANES] = ones_col

    @pl.when(j >= nqk + nv)
    def _():
        p_ref[...] = acc


def _rope_tables(pos):
    half = ROT_DIM // 2
    inv_freq = ROPE_THETA ** (-jnp.arange(0, ROT_DIM, 2, dtype=F32) / ROT_DIM)
    ang = pos[:, None] * inv_freq[None, :]
    cos, sin = jnp.cos(ang), jnp.sin(ang)
    r = pos.shape[0]
    ones = jnp.ones((r, HEAD_DIM - ROT_DIM), F32)
    zeros = jnp.zeros((r, HEAD_DIM - ROT_DIM), F32)
    zh = jnp.zeros((r, half), F32)
    c = jnp.concatenate([cos, cos, ones], axis=1)
    a = jnp.concatenate([-sin, zh, zeros], axis=1)
    b = jnp.concatenate([zh, sin, zeros], axis=1)
    rep = lambda t: jnp.concatenate([t, t], axis=1)
    return rep(c), rep(a), rep(b)


def _inproj(x2d, norm_w, w_in_bf, tables, rows_per_seq, tm, tn):
    n, d = x2d.shape
    d_in = w_in_bf.shape[1]
    attn_w = d // 2
    nqk = 2 * attn_w // tn
    nv = attn_w // tn
    npool = (d_in - 3 * attn_w) // tn
    tps = rows_per_seq // tm
    c, a, b = tables
    tab_spec = pl.BlockSpec((tm, LANES), lambda i, j: (i % tps, 0))
    kern = functools.partial(_inproj_kernel, nqk=nqk, nv=nv)
    return pl.pallas_call(
        kern,
        grid=(n // tm, d_in // tn),
        in_specs=[
            pl.BlockSpec((tm, d), lambda i, j: (i, 0)),
            pl.BlockSpec((1, d), lambda i, j: (0, 0)),
            pl.BlockSpec((d, tn), lambda i, j: (0, j)),
            tab_spec, tab_spec, tab_spec,
        ],
        out_specs=[
            pl.BlockSpec((tm, tn), lambda i, j: (i, jnp.minimum(j, nqk - 1))),
            pl.BlockSpec((tm, 2 * tn), lambda i, j: (i, jnp.clip(j - nqk, 0, nv - 1))),
            pl.BlockSpec((tm, tn), lambda i, j: (i, jnp.clip(j - nqk - nv, 0, npool - 1))),
        ],
        out_shape=[
            jax.ShapeDtypeStruct((n, 2 * attn_w), BF16),
            jax.ShapeDtypeStruct((n, 2 * attn_w), BF16),
            jax.ShapeDtypeStruct((n, d_in - 3 * attn_w), F32),
        ],
        scratch_shapes=[pltpu.VMEM((tm, d), BF16)],
        compiler_params=_cparams(("arbitrary", "arbitrary")),
        name="inproj",
    )(x2d, norm_w.reshape(1, d), w_in_bf, c, a, b)


def _lam_kernel(q1_ref, k1_ref, q2_ref, k2_ref, o_ref):
    s1 = jnp.sum(q1_ref[...] * k1_ref[...], axis=-1, keepdims=True)
    s2 = jnp.sum(q2_ref[...] * k2_ref[...], axis=-1, keepdims=True)
    o_ref[...] = jnp.exp(s1) - jnp.exp(s2) + LAMBDA_INIT


def _lam(lq1, lk1, lq2, lk2):
    r = lambda t: t.reshape(1, -1).astype(F32)
    return pl.pallas_call(
        _lam_kernel,
        out_shape=jax.ShapeDtypeStruct((1, 1), F32),
        name="lam",
    )(r(lq1), r(lk1), r(lq2), r(lk2))


def _attn_step(lam_ref, q_ref, k_ref, km_ref, v_ref, vm_ref, sw_ref, o_ref,
               s_w, sm_w, m_w, s_r, sm_r, m_r, oe_w, oe_r):
    tq = q_ref.shape[0]

    oe = oe_r[...]
    r = 1.0 / oe[:, V_DIM:V_DIM + 1]
    on = oe[:, :V_DIM] * r
    o = on[:tq] - lam_ref[...] * on[tq:]
    ms = jnp.mean(o * o, axis=-1, keepdims=True)
    y = o * lax.rsqrt(ms + SUBLN_EPS) * sw_ref[...]
    o_ref[...] = (y * (1.0 - LAMBDA_INIT)).astype(BF16)

    m_p = m_r[:, :1]
    e = jnp.exp(s_r[...] - m_p).astype(BF16)
    em = jnp.exp(sm_r[...] - m_p).astype(BF16)
    oe_w[...] = (jnp.dot(e, v_ref[...], preferred_element_type=F32)
                 + jnp.dot(em, vm_ref[...], preferred_element_type=F32))

    q = q_ref[...]
    lane = lax.broadcasted_iota(jnp.int32, q.shape, 1)
    zero = jnp.zeros_like(q)
    qs = jnp.concatenate([jnp.where(lane < HEAD_DIM, q, zero),
                          jnp.where(lane >= HEAD_DIM, q, zero)], axis=0)
    dn = (((1,), (1,)), ((), ()))
    s = lax.dot_general(qs, k_ref[...], dn, preferred_element_type=F32)
    sm = lax.dot_general(qs, km_ref[...], dn, preferred_element_type=F32)
    col = lax.broadcasted_iota(jnp.int32, sm.shape, 1)
    sm = jnp.where(col < N_META, sm, MASK_VALUE)
    m = jnp.maximum(jnp.max(s, axis=-1, keepdims=True), jnp.max(sm, axis=-1, keepdims=True))
    s_w[...] = s
    sm_w[...] = sm
    m_w[...] = jnp.broadcast_to(m, m_w.shape)


def _attn_kernel(lam_ref, q_ref, k_ref, km_ref, v_ref, vm_ref, sw_ref, o_ref,
                 s_a, s_b, sm_a, sm_b, m_a, m_b, oe_a, oe_b):
    t = pl.program_id(0)
    args = (lam_ref, q_ref, k_ref, km_ref, v_ref, vm_ref, sw_ref, o_ref)

    @pl.when(t == 0)
    def _():
        s_b[...] = jnp.zeros_like(s_b)
        sm_b[...] = jnp.zeros_like(sm_b)
        m_b[...] = jnp.zeros_like(m_b)
        oe_a[...] = jnp.ones_like(oe_a)

    @pl.when(t % 2 == 0)
    def _():
        _attn_step(*args, s_a, sm_a, m_a, s_b, sm_b, m_b, oe_b, oe_a)

    @pl.when(t % 2 == 1)
    def _():
        _attn_step(*args, s_b, sm_b, m_b, s_a, sm_a, m_a, oe_a, oe_b)


def _attention(lam, qk, vx, qk_meta, vx_meta, subln_w, batch, seq, tq):
    n, two_w = qk.shape
    n_heads = two_w // 2 // V_DIM
    nqt = seq // tq
    units = batch * n_heads * nqt
    vw = 2 * V_DIM

    def unit(u):
        bh = u // nqt
        return bh // n_heads, bh % n_heads, u % nqt

    def cur(t):
        return unit(jnp.minimum(t, units - 1))

    def prv(t):
        return unit(jnp.clip(t - 1, 0, units - 1))

    def prv2(t):
        return unit(jnp.maximum(t - 2, 0))

    def q_map(t):
        b, h, i = cur(t)
        return (b * nqt + i, h)

    def k_map(t):
        b, h, _ = cur(t)
        return (b, n_heads + h)

    def km_map(t):
        _, h, _ = cur(t)
        return (0, n_heads + h)

    def v_map(t):
        b, h, _ = prv(t)
        return (b, h)

    def vm_map(t):
        _, h, _ = prv(t)
        return (0, h)

    def o_map(t):
        b, h, i = prv2(t)
        return (b * nqt + i, h)

    return pl.pallas_call(
        _attn_kernel,
        grid=(units + 2,),
        in_specs=[
            pl.BlockSpec((1, 1), lambda t: (0, 0)),
            pl.BlockSpec((tq, V_DIM), q_map),
            pl.BlockSpec((seq, V_DIM), k_map),
            pl.BlockSpec((META_ROWS, V_DIM), km_map),
            pl.BlockSpec((seq, vw), v_map),
            pl.BlockSpec((META_ROWS, vw), vm_map),
            pl.BlockSpec((1, V_DIM), lambda t: (0, 0)),
        ],
        out_specs=pl.BlockSpec((tq, V_DIM), o_map),
        out_shape=jax.ShapeDtypeStruct((n, two_w // 2), BF16),
        scratch_shapes=(
            [pltpu.VMEM((2 * tq, seq), F32)] * 2
            + [pltpu.VMEM((2 * tq, META_ROWS), F32)] * 2
            + [pltpu.VMEM((2 * tq, LANES), F32)] * 2
            + [pltpu.VMEM((2 * tq, vw), F32)] * 2
        ),
        compiler_params=_cparams(("arbitrary",)),
        name="attn",
    )(lam, qk, qk, qk_meta, vx, vx_meta, subln_w.reshape(1, V_DIM).astype(F32))


def _pool_kernel(p_ref, prev_ref, next_ref, pm_ref, w_ref, sc_ref, o_ref, *, tps, total_t):
    i = pl.program_id(0)
    tm = p_ref.shape[0]
    ti = i % tps
    p = p_ref[...]
    prev = jnp.where(ti == 0, pm_ref[...], prev_ref[...])
    nxt = jnp.where(ti == tps - 1, 0.0, next_ref[...])
    ext = jnp.concatenate([prev, p, nxt], axis=0)
    rows = tm + 2 * HALO
    shift = lambda t, k: pltpu.roll(t, rows - k, 0)
    t_pos = N_META + ti * tm + lax.broadcasted_iota(jnp.int32, (tm, 1), 0)
    pch = p.shape[1] // len(POOL_WINDOWS)
    for g, w in enumerate(POOL_WINDOWS):
        sl = slice(g * pch, (g + 1) * pch)
        acc = ext[:, sl]
        span = 1
        while span < w:
            acc = acc + shift(acc, span)
            span *= 2
        off = HALO - w // 2
        win = (shift(acc, off) if off else acc)[:tm]
        cnt = jnp.minimum(w, total_t - t_pos + w // 2).astype(F32)
        z = (win / cnt - p[:, sl]).astype(BF16)
        y = jnp.dot(z, w_ref[g], preferred_element_type=F32)
        o_ref[:, sl] = (y * sc_ref[:, sl]).astype(BF16)


def _pool(p, p_meta, pool_w_bf, pool_scale, seq, tm):
    n, pw = p.shape
    tps = seq // tm
    hb = tm // HALO
    nb = n // HALO
    kern = functools.partial(_pool_kernel, tps=tps, total_t=seq + N_META)
    return pl.pallas_call(
        kern,
        grid=(n // tm,),
        in_specs=[
            pl.BlockSpec((tm, pw), lambda i: (i, 0)),
            pl.BlockSpec((HALO, pw), lambda i: (jnp.maximum(i * hb - 1, 0), 0)),
            pl.BlockSpec((HALO, pw), lambda i: (jnp.minimum((i + 1) * hb, nb - 1), 0)),
            pl.BlockSpec((HALO, pw), lambda i: (N_META // HALO - 1, 0)),
            pl.BlockSpec(pool_w_bf.shape, lambda i: (0, 0, 0)),
            pl.BlockSpec((1, pw), lambda i: (0, 0)),
        ],
        out_specs=pl.BlockSpec((tm, pw), lambda i: (i, 0)),
        out_shape=jax.ShapeDtypeStruct((n, pw), BF16),
        compiler_params=_cparams(("arbitrary",)),
        name="pool",
    )(p, p, p, p_meta, pool_w_bf, pool_scale.reshape(1, pw).astype(F32))


def _outproj_kernel(a_ref, p_ref, wa_ref, wp_ref, x_ref, o_ref):
    acc = jnp.dot(a_ref[...], wa_ref[...], preferred_element_type=F32)
    acc = acc + jnp.dot(p_ref[...], wp_ref[...], preferred_element_type=F32)
    o_ref[...] = x_ref[...] + acc


def _outproj(attn, pool, w_out_bf, x2d, tm, tn):
    n, d = x2d.shape
    half = attn.shape[1]
    return pl.pallas_call(
        _outproj_kernel,
        grid=(n // tm, d // tn),
        in_specs=[
            pl.BlockSpec((tm, half), lambda i, j: (i, 0)),
            pl.BlockSpec((tm, half), lambda i, j: (i, 0)),
            pl.BlockSpec((half, tn), lambda i, j: (0, j)),
            pl.BlockSpec((half, tn), lambda i, j: (1, j)),
            pl.BlockSpec((tm, tn), lambda i, j: (i, j)),
        ],
        out_specs=pl.BlockSpec((tm, tn), lambda i, j: (i, j)),
        out_shape=jax.ShapeDtypeStruct((n, d), F32),
        compiler_params=_cparams(("arbitrary", "arbitrary")),
        name="outproj",
    )(attn, pool, w_out_bf, w_out_bf, x2d)


def _router_kernel(h_ref, nw_ref, wr_ref, hn_ref, lg_ref):
    h = h_ref[...]
    ms = jnp.mean(h * h, axis=-1, keepdims=True)
    hn = h * lax.rsqrt(ms + NORM_EPS) * nw_ref[...]
    hn_ref[...] = hn
    lg_ref[...] = jnp.dot(hn.astype(BF16), wr_ref[...], preferred_element_type=F32)


def _router(h1, norm_w, wr_bf, tm):
    n, d = h1.shape
    return pl.pallas_call(
        _router_kernel,
        grid=(n // tm,),
        in_specs=[
            pl.BlockSpec((tm, d), lambda i: (i, 0)),
            pl.BlockSpec((1, d), lambda i: (0, 0)),
            pl.BlockSpec((d, LANES), lambda i: (0, 0)),
        ],
        out_specs=[
            pl.BlockSpec((tm, d), lambda i: (i, 0)),
            pl.BlockSpec((tm, LANES), lambda i: (i, 0)),
        ],
        out_shape=[
            jax.ShapeDtypeStruct((n, d), F32),
            jax.ShapeDtypeStruct((n, LANES), F32),
        ],
        compiler_params=_cparams(("arbitrary",)),
        name="router",
    )(h1, norm_w.reshape(1, d), wr_bf)


def _route(logits, bm):
    n = logits.shape[0]
    gl = logits[:, :N_GROUPS]
    el = logits[:, N_GROUPS:N_GROUPS + N_EXPERTS].reshape(n, N_GROUPS, EXPERTS_PER_GROUP)
    g_prob = jax.nn.softmax(gl, axis=-1)
    g_sel = jnp.argmax(gl, axis=-1).astype(jnp.int32)
    p_g = jnp.take_along_axis(g_prob, g_sel[:, None], axis=1)[:, 0]
    esl = jnp.take_along_axis(el, g_sel[:, None, None], axis=1)[:, 0]
    top_v, top_i = lax.top_k(esl, TOP_K)
    top_w = jax.nn.softmax(top_v, axis=-1) * p_g[:, None]
    eid = (g_sel[:, None] * EXPERTS_PER_GROUP + top_i).reshape(-1).astype(jnp.int32)
    wts = top_w.reshape(-1)
    na = n * TOP_K
    onehot = (eid[:, None] == jnp.arange(N_EXPERTS, dtype=jnp.int32)[None, :]).astype(jnp.int32)
    csum = jnp.cumsum(onehot, axis=0)
    rank = jnp.take_along_axis(csum, eid[:, None], axis=1)[:, 0] - 1
    counts = csum[-1]
    pcounts = ((counts + bm - 1) // bm) * bm
    pends = jnp.cumsum(pcounts)
    pstarts = pends - pcounts
    dest = pstarts[eid] + rank
    n_blocks = -(-na // bm) + N_EXPERTS
    rows = n_blocks * bm
    aidx = jnp.arange(na, dtype=jnp.int32)
    inv = jnp.full((rows,), -1, jnp.int32).at[dest].set(aidx)
    valid = inv >= 0
    src = jnp.maximum(inv, 0)
    tok = src // TOP_K
    slot = jnp.where(valid, (src % TOP_K) * n + src // TOP_K,
                     na + jnp.arange(rows, dtype=jnp.int32) % bm)
    wrow = jnp.where(valid, wts[src], 0.0)
    bstart = jnp.arange(n_blocks, dtype=jnp.int32) * bm
    block_e = jnp.minimum(jnp.sum(bstart[:, None] >= pends[None, :], axis=1),
                          N_EXPERTS - 1).astype(jnp.int32)
    nvalid = jnp.clip(pstarts[block_e] + counts[block_e] - bstart, 0, bm).astype(jnp.int32)
    return block_e, nvalid, tok, slot, wrow.reshape(rows, 1), n_blocks


def _start_row_gather(tok_ref, x_hbm, buf, sem, blk):
    bm = buf.shape[0]
    for r in range(bm):
        t = tok_ref[blk * bm + r]
        pltpu.make_async_copy(x_hbm.at[pl.ds(t, 1)], buf.at[pl.ds(r, 1)], sem).start()


def _wait_row_gather(x_hbm, buf, sem):
    for r in range(buf.shape[0]):
        pltpu.make_async_copy(x_hbm.at[pl.ds(0, 1)], buf.at[pl.ds(r, 1)], sem).wait()


def _gateup_kernel(be_ref, nv_ref, tok_ref, x_hbm, wg_ref, wu_ref, h_ref, xa, xb, sem, *, nb):
    i = pl.program_id(0)
    nv = nv_ref[i]
    nv_prev = nv_ref[jnp.maximum(i - 1, 0)]
    bufs = ((xa, sem.at[0]), (xb, sem.at[1]))

    @pl.when(i == 0)
    def _():
        _start_row_gather(tok_ref, x_hbm, xa, sem.at[0], 0)

    def step(cur, nxt):
        _wait_row_gather(x_hbm, *cur)
        _start_row_gather(tok_ref, x_hbm, *nxt, jnp.minimum(i + 1, nb - 1))
        x = cur[0][...].astype(BF16)
        g = jnp.dot(x, wg_ref[0], preferred_element_type=F32)
        u = jnp.dot(x, wu_ref[0], preferred_element_type=F32)
        h_ref[...] = (g * (1.0 / (1.0 + jnp.exp(-g))) * u).astype(BF16)

    for p in (0, 1):
        @pl.when((i % 2 == p) & (nv > 0))
        def _(p=p):
            step(bufs[p], bufs[1 - p])

        @pl.when((i % 2 == p) & (nv == 0) & (nv_prev > 0))
        def _(p=p):
            _wait_row_gather(x_hbm, *bufs[p])

    @pl.when((i == nb - 1) & (nv > 0))
    def _():
        _wait_row_gather(x_hbm, *bufs[nb % 2])

    @pl.when(nv == 0)
    def _():
        h_ref[...] = jnp.zeros_like(h_ref)


def _gateup(block_e, nvalid, tok, hn2, wg_bf, wu_bf, n_blocks, bm):
    n, d = hn2.shape
    f = wg_bf.shape[2]
    grid_spec = pltpu.PrefetchScalarGridSpec(
        num_scalar_prefetch=3,
        grid=(n_blocks,),
        in_specs=[
            pl.BlockSpec(memory_space=pl.ANY),
            pl.BlockSpec((1, d, f), lambda i, be, nv, tk: (be[i], 0, 0)),
            pl.BlockSpec((1, d, f), lambda i, be, nv, tk: (be[i], 0, 0)),
        ],
        out_specs=pl.BlockSpec((bm, f), lambda i, be, nv, tk: (i, 0)),
        scratch_shapes=[pltpu.VMEM((bm, d), F32), pltpu.VMEM((bm, d), F32),
                        pltpu.SemaphoreType.DMA((2,))],
    )
    return pl.pallas_call(
        functools.partial(_gateup_kernel, nb=n_blocks),
        grid_spec=grid_spec,
        out_shape=jax.ShapeDtypeStruct((n_blocks * bm, f), BF16),
        compiler_params=_cparams(("arbitrary",)),
        name="gateup",
    )(block_e, nvalid, tok, hn2, wg_bf, wu_bf)


def _start_row_scatter(slot_ref, buf, y_hbm, sem, blk):
    bm = buf.shape[0]
    for r in range(bm):
        s = slot_ref[blk * bm + r]
        pltpu.make_async_copy(buf.at[pl.ds(r, 1)], y_hbm.at[pl.ds(s, 1)], sem).start()


def _wait_row_scatter(buf, y_hbm, sem):
    for r in range(buf.shape[0]):
        pltpu.make_async_copy(buf.at[pl.ds(r, 1)], y_hbm.at[pl.ds(0, 1)], sem).wait()


def _down_kernel(be_ref, nv_ref, slot_ref, h_ref, wd_ref, w_ref, y_hbm, ya, yb, sem, *, nb):
    i = pl.program_id(0)
    nv = nv_ref[i]
    nv_prev = nv_ref[jnp.maximum(i - 1, 0)]
    bufs = ((ya, sem.at[0]), (yb, sem.at[1]))

    def compute(buf):
        y = jnp.dot(h_ref[...], wd_ref[0], preferred_element_type=F32)
        buf[...] = y * w_ref[...]

    @pl.when(i == 0)
    def _():
        bm, na = yb.shape[0], y_hbm.shape[0] - yb.shape[0]
        yb[...] = jnp.zeros_like(yb)
        spare = pltpu.make_async_copy(yb, y_hbm.at[pl.ds(na, bm)], sem.at[1])
        spare.start()
        spare.wait()
        compute(ya)

    for p in (0, 1):
        cur, oth = bufs[p], bufs[1 - p]

        @pl.when((i % 2 == p) & (i >= 2) & (nv_prev > 0))
        def _(cur=cur):
            _wait_row_scatter(cur[0], y_hbm, cur[1])

        @pl.when((i % 2 == p) & (i >= 1) & (nv > 0))
        def _(cur=cur, oth=oth):
            _start_row_scatter(slot_ref, oth[0], y_hbm, oth[1], i - 1)
            compute(cur[0])

        @pl.when((i % 2 == p) & (nv == 0) & (nv_prev > 0))
        def _(oth=oth):
            _start_row_scatter(slot_ref, oth[0], y_hbm, oth[1], i - 1)
            _wait_row_scatter(oth[0], y_hbm, oth[1])

    @pl.when((i == nb - 1) & (nv > 0))
    def _():
        cur, oth = bufs[(nb - 1) % 2], bufs[nb % 2]
        if nb > 1:
            _wait_row_scatter(oth[0], y_hbm, oth[1])
        _start_row_scatter(slot_ref, cur[0], y_hbm, cur[1], nb - 1)
        _wait_row_scatter(cur[0], y_hbm, cur[1])


def _down(block_e, nvalid, slot, h, wd_bf, wrow, n_assign, n_blocks, bm):
    f, d = wd_bf.shape[1], wd_bf.shape[2]
    grid_spec = pltpu.PrefetchScalarGridSpec(
        num_scalar_prefetch=3,
        grid=(n_blocks,),
        in_specs=[
            pl.BlockSpec((bm, f), lambda i, be, nv, sl: (i, 0)),
            pl.BlockSpec((1, f, d), lambda i, be, nv, sl: (be[i], 0, 0)),
            pl.BlockSpec((bm, 1), lambda i, be, nv, sl: (i, 0)),
        ],
        out_specs=pl.BlockSpec(memory_space=pl.ANY),
        scratch_shapes=[pltpu.VMEM((bm, d), F32), pltpu.VMEM((bm, d), F32),
                        pltpu.SemaphoreType.DMA((2,))],
    )
    return pl.pallas_call(
        functools.partial(_down_kernel, nb=n_blocks),
        grid_spec=grid_spec,
        out_shape=jax.ShapeDtypeStruct((n_assign + bm, d), F32),
        compiler_params=_cparams(("arbitrary",)),
        name="down",
    )(block_e, nvalid, slot, h, wd_bf, wrow)


def _final_kernel(h_ref, y0_ref, y1_ref, nw_ref, o_ref):
    h = h_ref[...] + (y0_ref[...] + y1_ref[...])
    ms = jnp.mean(h * h, axis=-1, keepdims=True)
    o_ref[...] = h * lax.rsqrt(ms + NORM_EPS) * nw_ref[...]


def _final(h1, yc, norm_w, tm):
    n, d = h1.shape
    nt = n // tm
    return pl.pallas_call(
        _final_kernel,
        grid=(nt,),
        in_specs=[
            pl.BlockSpec((tm, d), lambda i: (i, 0)),
            pl.BlockSpec((tm, d), lambda i: (i, 0)),
            pl.BlockSpec((tm, d), lambda i: (i + nt, 0)),
            pl.BlockSpec((1, d), lambda i: (0, 0)),
        ],
        out_specs=pl.BlockSpec((tm, d), lambda i: (i, 0)),
        out_shape=jax.ShapeDtypeStruct((n, d), F32),
        compiler_params=_cparams(("arbitrary",)),
        name="final",
    )(h1, yc, yc, norm_w.reshape(1, d))


def _tiles(seq, d):
    tm = min(512, seq)
    return dict(tm=tm, tn=min(512, d // 2), tq=min(256, seq), tm_out=tm,
                tn_out=min(1024, d), tm_fin=min(256, seq), bm=min(256, seq))


def _trunk(x, shared, prm):
    b, seq, d = x.shape
    n = b * seq
    t = _tiles(seq, d)
    x2d = x.reshape(n, d)
    qk, v, p = _inproj(x2d, prm["attn_norm_w"], shared["w_in"], shared["rope_real"], seq,
                       t["tm"], t["tn"])
    attn = _attention(shared["lam"], qk, v, shared["qk_meta"], shared["v_meta"],
                      prm["subln_w"], b, seq, t["tq"])
    pool = _pool(p, shared["p_meta"], shared["pool_w"], prm["pool_scale"], seq, t["tm"])
    h1 = _outproj(attn, pool, shared["w_out"], x2d, t["tm_out"], t["tn_out"])
    hn2, logits = _router(h1, prm["ffn_norm_w"], shared["w_router"], t["tm"])
    bm = t["bm"]
    block_e, nvalid, tok, slot, wrow, n_blocks = _route(logits, bm)
    h = _gateup(block_e, nvalid, tok, hn2, shared["w_gate"], shared["w_up"], n_blocks, bm)
    yc = _down(block_e, nvalid, slot, h, shared["w_down"], wrow, n * TOP_K, n_blocks, bm)
    out = _final(h1, yc, prm["final_norm_w"], t["tm_fin"])
    return out.reshape(b, seq, d)


def kernel(x_prompt, x_sample, meta_tokens, attn_norm_w, w_in, lambda_q1, lambda_k1, lambda_q2, lambda_k2, subln_w, pool_w, pool_scale, w_out, ffn_norm_w, router_group, router_expert, expert_w_gate, expert_w_up, expert_w_down, final_norm_w):
    d = x_prompt.shape[-1]
    seq = x_prompt.shape[1]
    assert x_sample.shape[1] == seq and attn_norm_w.shape[0] == 1
    prm = dict(attn_norm_w=attn_norm_w[0], subln_w=subln_w[0], pool_scale=pool_scale[0],
               ffn_norm_w=ffn_norm_w[0], final_norm_w=final_norm_w)
    wr = jnp.concatenate([router_group[0], router_expert[0]], axis=1)
    wr = jnp.pad(wr, ((0, 0), (0, LANES - wr.shape[1])))
    shared = dict(
        w_in=w_in[0].astype(BF16), w_out=w_out[0].astype(BF16), pool_w=pool_w[0].astype(BF16),
        w_router=wr.astype(BF16), w_gate=expert_w_gate[0].astype(BF16),
        w_up=expert_w_up[0].astype(BF16), w_down=expert_w_down[0].astype(BF16),
        lam=_lam(lambda_q1[0], lambda_k1[0], lambda_q2[0], lambda_k2[0]),
        rope_real=_rope_tables(jnp.arange(N_META, N_META + seq, dtype=F32)),
    )
    meta_pad = jnp.pad(meta_tokens.astype(F32), ((0, META_ROWS - N_META), (0, 0)))
    t = _tiles(seq, d)
    qk_m, v_m, p_m = _inproj(meta_pad, prm["attn_norm_w"], shared["w_in"],
                             _rope_tables(jnp.arange(META_ROWS, dtype=F32)), META_ROWS,
                             META_ROWS, t["tn"])
    shared.update(qk_meta=qk_m, v_meta=v_m, p_meta=p_m)
    return (_trunk(x_prompt, shared, prm), _trunk(x_sample, shared, prm))
```

```python
import functools
import math

import jax
import jax.numpy as jnp
from jax import lax
from jax.experimental import pallas as pl
from jax.experimental.pallas import tpu as pltpu

F32 = jnp.float32
BF16 = jnp.bfloat16

N_META = 16
HEAD_DIM = 64
V_DIM = 2 * HEAD_DIM
ROT_DIM = HEAD_DIM // 4
ROPE_THETA = 500000.0
ATTN_SCALE = HEAD_DIM ** -0.5
SUBLN_EPS = 1e-5
NORM_EPS = 1e-6
POOL_WINDOWS = (2, 4, 8, 16)
N_GROUPS = 4
EXPERTS_PER_GROUP = 8
N_EXPERTS = N_GROUPS * EXPERTS_PER_GROUP
TOP_K = 2
LAMBDA_INIT = 0.8 - 0.6 * math.exp(-0.3 * 0)

LANES = 128
SUBLANES = 8
META_ROWS = 128
HALO = 8
MASK_VALUE = -1e30

VMEM_LIMIT = 56 * 1024 * 1024


def _cparams(sem):
    return pltpu.CompilerParams(dimension_semantics=sem, vmem_limit_bytes=VMEM_LIMIT)


def _inproj_kernel(x_ref, nw_ref, w_ref, c_ref, s_ref,
                   qk_ref, v_ref, p_ref, hn_ref, *, nqk, nv):
    j = pl.program_id(1)

    @pl.when(j == 0)
    def _():
        x = x_ref[...]
        ms = jnp.mean(x * x, axis=-1, keepdims=True)
        hn_ref[...] = (x * lax.rsqrt(ms + NORM_EPS) * nw_ref[...]).astype(BF16)

    acc = jnp.dot(hn_ref[...], w_ref[...], preferred_element_type=F32)

    @pl.when(j < nqk)
    def _():
        scale = jnp.where(j < nqk // 2, ATTN_SCALE, 1.0).astype(F32)
        c = c_ref[...] * scale
        s = s_ref[...] * scale
        for t in range(acc.shape[1] // LANES):
            blk = acc[:, t * LANES:(t + 1) * LANES]
            r = blk * c + pltpu.roll(blk, LANES // 2, 1) * s
            qk_ref[:, t * LANES:(t + 1) * LANES] = r.astype(BF16)

    @pl.when((j >= nqk) & (j < nqk + nv))
    def _():
        lane = lax.broadcasted_iota(jnp.int32, (acc.shape[0], LANES), 1)
        ones_col = jnp.where(lane == 0, 1.0, 0.0).astype(BF16)
        for t in range(acc.shape[1] // LANES):
            v_ref[:, 2 * t * LANES:(2 * t + 1) * LANES] = acc[:, t * LANES:(t + 1) * LANES].astype(BF16)
            v_ref[:, (2 * t + 1) * LANES:(2 * t + 2) * LANES] = ones_col

    @pl.when(j >= nqk + nv)
    def _():
        p_ref[...] = acc


_HALF = ROT_DIM // 2
HEAD_LANE_SRC = (list(range(0, _HALF)) + list(range(HEAD_DIM, HEAD_DIM + _HALF))
                 + list(range(ROT_DIM, HEAD_DIM))
                 + list(range(_HALF, ROT_DIM)) + list(range(HEAD_DIM + _HALF, HEAD_DIM + ROT_DIM))
                 + list(range(HEAD_DIM + ROT_DIM, 2 * HEAD_DIM)))
COMPONENT0_LANES = tuple(i for i, src in enumerate(HEAD_LANE_SRC) if src < HEAD_DIM)


def _head_lane_order(w):
    lead = w.shape[:-1]
    w3 = w.reshape(lead + (w.shape[-1] // V_DIM, V_DIM))
    runs, start = [], 0
    for i in range(1, V_DIM + 1):
        if i == V_DIM or HEAD_LANE_SRC[i] != HEAD_LANE_SRC[i - 1] + 1:
            runs.append(w3[..., HEAD_LANE_SRC[start]:HEAD_LANE_SRC[i - 1] + 1])
            start = i
    return jnp.concatenate(runs, axis=-1).reshape(w.shape)


def _rope_tables(pos):
    inv_freq = ROPE_THETA ** (-jnp.arange(0, ROT_DIM, 2, dtype=F32) / ROT_DIM)
    ang = pos[:, None] * inv_freq[None, :]
    cos, sin = jnp.cos(ang), jnp.sin(ang)
    r = pos.shape[0]
    ones = jnp.ones((r, HEAD_DIM - ROT_DIM), F32)
    zeros = jnp.zeros((r, HEAD_DIM - ROT_DIM), F32)
    c = jnp.concatenate([cos, cos, ones, cos, cos, ones], axis=1)
    s = jnp.concatenate([-sin, -sin, zeros, sin, sin, zeros], axis=1)
    return c, s


def _inproj(x2d, norm_w, w_in_bf, tables, rows_per_seq, tm, tn):
    n, d = x2d.shape
    d_in = w_in_bf.shape[1]
    attn_w = d // 2
    nqk = 2 * attn_w // tn
    nv = attn_w // tn
    npool = (d_in - 3 * attn_w) // tn
    tps = rows_per_seq // tm
    c, s = tables
    tab_spec = pl.BlockSpec((tm, LANES), lambda i, j: (i % tps, 0))
    kern = functools.partial(_inproj_kernel, nqk=nqk, nv=nv)
    return pl.pallas_call(
        kern,
        grid=(n // tm, d_in // tn),
        in_specs=[
            pl.BlockSpec((tm, d), lambda i, j: (i, 0)),
            pl.BlockSpec((1, d), lambda i, j: (0, 0)),
            pl.BlockSpec((d, tn), lambda i, j: (0, j)),
            tab_spec, tab_spec,
        ],
        out_specs=[
            pl.BlockSpec((tm, tn), lambda i, j: (i, jnp.minimum(j, nqk - 1))),
            pl.BlockSpec((tm, 2 * tn), lambda i, j: (i, jnp.clip(j - nqk, 0, nv - 1))),
            pl.BlockSpec((tm, tn), lambda i, j: (i, jnp.clip(j - nqk - nv, 0, npool - 1))),
        ],
        out_shape=[
            jax.ShapeDtypeStruct((n, 2 * attn_w), BF16),
            jax.ShapeDtypeStruct((n, 2 * attn_w), BF16),
            jax.ShapeDtypeStruct((n, d_in - 3 * attn_w), F32),
        ],
        scratch_shapes=[pltpu.VMEM((tm, d), BF16)],
        compiler_params=_cparams(("arbitrary", "arbitrary")),
        name="inproj",
    )(x2d, norm_w.reshape(1, d), w_in_bf, c, s)


def _lam_kernel(q1_ref, k1_ref, q2_ref, k2_ref, o_ref):
    s1 = jnp.sum(q1_ref[...] * k1_ref[...], axis=-1, keepdims=True)
    s2 = jnp.sum(q2_ref[...] * k2_ref[...], axis=-1, keepdims=True)
    o_ref[...] = jnp.exp(s1) - jnp.exp(s2) + LAMBDA_INIT


def _lam(lq1, lk1, lq2, lk2):
    r = lambda t: t.reshape(1, -1).astype(F32)
    return pl.pallas_call(
        _lam_kernel,
        out_shape=jax.ShapeDtypeStruct((1, 1), F32),
        name="lam",
    )(r(lq1), r(lk1), r(lq2), r(lk2))


def _attn_step(lam_ref, q_ref, k_ref, km_ref, v_ref, vm_ref, sw_ref, o_ref,
               s_w, sm_w, m_w, s_r, sm_r, m_r, oe_w, oe_r):
    tq = q_ref.shape[0]

    oe = oe_r[...]
    r = 1.0 / oe[:, V_DIM:V_DIM + 1]
    on = oe[:, :V_DIM] * r
    o = on[:tq] - lam_ref[...] * on[tq:]
    ms = jnp.mean(o * o, axis=-1, keepdims=True)
    y = o * lax.rsqrt(ms + SUBLN_EPS) * sw_ref[...]
    o_ref[...] = (y * (1.0 - LAMBDA_INIT)).astype(BF16)

    m_p = m_r[:, :1]
    e = jnp.exp(s_r[...] - m_p).astype(BF16)
    em = jnp.exp(sm_r[...] - m_p).astype(BF16)
    oe_w[...] = (jnp.dot(e, v_ref[...], preferred_element_type=F32)
                 + jnp.dot(em, vm_ref[...], preferred_element_type=F32))

    q = q_ref[...]
    lane = lax.broadcasted_iota(jnp.int32, q.shape, 1)
    zero = jnp.zeros_like(q)
    assert COMPONENT0_LANES == tuple(range(0, 8)) + tuple(range(16, 72))
    comp0 = (lane < 8) | ((lane >= 16) & (lane < 72))
    qs = jnp.concatenate([jnp.where(comp0, q, zero),
                          jnp.where(comp0, zero, q)], axis=0)
    dn = (((1,), (1,)), ((), ()))
    s = lax.dot_general(qs, k_ref[...], dn, preferred_element_type=F32)
    sm = lax.dot_general(qs, km_ref[...], dn, preferred_element_type=F32)
    col = lax.broadcasted_iota(jnp.int32, sm.shape, 1)
    sm = jnp.where(col < N_META, sm, MASK_VALUE)
    m = jnp.maximum(jnp.max(s, axis=-1, keepdims=True), jnp.max(sm, axis=-1, keepdims=True))
    s_w[...] = s
    sm_w[...] = sm
    m_w[...] = jnp.broadcast_to(m, m_w.shape)


def _attn_kernel(lam_ref, q_ref, k_ref, km_ref, v_ref, vm_ref, sw_ref, o_ref,
                 s_a, s_b, sm_a, sm_b, m_a, m_b, oe_a, oe_b):
    t = pl.program_id(0)
    args = (lam_ref, q_ref, k_ref, km_ref, v_ref, vm_ref, sw_ref, o_ref)

    @pl.when(t == 0)
    def _():
        s_b[...] = jnp.zeros_like(s_b)
        sm_b[...] = jnp.zeros_like(sm_b)
        m_b[...] = jnp.zeros_like(m_b)
        oe_a[...] = jnp.ones_like(oe_a)

    @pl.when(t % 2 == 0)
    def _():
        _attn_step(*args, s_a, sm_a, m_a, s_b, sm_b, m_b, oe_b, oe_a)

    @pl.when(t % 2 == 1)
    def _():
        _attn_step(*args, s_b, sm_b, m_b, s_a, sm_a, m_a, oe_a, oe_b)


def _attention(lam, qk, vx, qk_meta, vx_meta, subln_w, batch, seq, tq):
    n, two_w = qk.shape
    n_heads = two_w // 2 // V_DIM
    nqt = seq // tq
    units = batch * n_heads * nqt
    vw = 2 * V_DIM

    def unit(u):
        bh = u // nqt
        return bh // n_heads, bh % n_heads, u % nqt

    def cur(t):
        return unit(jnp.minimum(t, units - 1))

    def prv(t):
        return unit(jnp.clip(t - 1, 0, units - 1))

    def prv2(t):
        return unit(jnp.maximum(t - 2, 0))

    def q_map(t):
        b, h, i = cur(t)
        return (b * nqt + i, h)

    def k_map(t):
        b, h, _ = cur(t)
        return (b, n_heads + h)

    def km_map(t):
        _, h, _ = cur(t)
        return (0, n_heads + h)

    def v_map(t):
        b, h, _ = prv(t)
        return (b, h)

    def vm_map(t):
        _, h, _ = prv(t)
        return (0, h)

    def o_map(t):
        b, h, i = prv2(t)
        return (b * nqt + i, h)

    return pl.pallas_call(
        _attn_kernel,
        grid=(units + 2,),
        in_specs=[
            pl.BlockSpec((1, 1), lambda t: (0, 0)),
            pl.BlockSpec((tq, V_DIM), q_map),
            pl.BlockSpec((seq, V_DIM), k_map),
            pl.BlockSpec((META_ROWS, V_DIM), km_map),
            pl.BlockSpec((seq, vw), v_map),
            pl.BlockSpec((META_ROWS, vw), vm_map),
            pl.BlockSpec((1, V_DIM), lambda t: (0, 0)),
        ],
        out_specs=pl.BlockSpec((tq, V_DIM), o_map),
        out_shape=jax.ShapeDtypeStruct((n, two_w // 2), BF16),
        scratch_shapes=(
            [pltpu.VMEM((2 * tq, seq), F32)] * 2
            + [pltpu.VMEM((2 * tq, META_ROWS), F32)] * 2
            + [pltpu.VMEM((2 * tq, LANES), F32)] * 2
            + [pltpu.VMEM((2 * tq, vw), F32)] * 2
        ),
        compiler_params=_cparams(("arbitrary",)),
        name="attn",
    )(lam, qk, qk, qk_meta, vx, vx_meta, subln_w.reshape(1, V_DIM).astype(F32))


def _pool_kernel(p_ref, prev_ref, next_ref, pm_ref, w_ref, sc_ref, o_ref, *, tps, total_t):
    i = pl.program_id(0)
    tm = p_ref.shape[0]
    ti = i % tps
    p = p_ref[...]
    prev = jnp.where(ti == 0, pm_ref[...], prev_ref[...])
    nxt = jnp.where(ti == tps - 1, 0.0, next_ref[...])
    ext = jnp.concatenate([prev, p, nxt], axis=0)
    rows = tm + 2 * HALO
    shift = lambda t, k: pltpu.roll(t, rows - k, 0)
    t_pos = N_META + ti * tm + lax.broadcasted_iota(jnp.int32, (tm, 1), 0)
    pch = p.shape[1] // len(POOL_WINDOWS)
    for g, w in enumerate(POOL_WINDOWS):
        sl = slice(g * pch, (g + 1) * pch)
        acc = ext[:, sl]
        span = 1
        while span < w:
            acc = acc + shift(acc, span)
            span *= 2
        off = HALO - w // 2
        win = (shift(acc, off) if off else acc)[:tm]
        cnt = jnp.minimum(w, total_t - t_pos + w // 2).astype(F32)
        z = (win / cnt - p[:, sl]).astype(BF16)
        y = jnp.dot(z, w_ref[g], preferred_element_type=F32)
        o_ref[:, sl] = (y * sc_ref[:, sl]).astype(BF16)


def _pool(p, p_meta, pool_w_bf, pool_scale, seq, tm):
    n, pw = p.shape
    tps = seq // tm
    hb = tm // HALO
    nb = n // HALO
    kern = functools.partial(_pool_kernel, tps=tps, total_t=seq + N_META)
    return pl.pallas_call(
        kern,
        grid=(n // tm,),
        in_specs=[
            pl.BlockSpec((tm, pw), lambda i: (i, 0)),
            pl.BlockSpec((HALO, pw), lambda i: (jnp.maximum(i * hb - 1, 0), 0)),
            pl.BlockSpec((HALO, pw), lambda i: (jnp.minimum((i + 1) * hb, nb - 1), 0)),
            pl.BlockSpec((HALO, pw), lambda i: (N_META // HALO - 1, 0)),
            pl.BlockSpec(pool_w_bf.shape, lambda i: (0, 0, 0)),
            pl.BlockSpec((1, pw), lambda i: (0, 0)),
        ],
        out_specs=pl.BlockSpec((tm, pw), lambda i: (i, 0)),
        out_shape=jax.ShapeDtypeStruct((n, pw), BF16),
        compiler_params=_cparams(("arbitrary",)),
        name="pool",
    )(p, p, p, p_meta, pool_w_bf, pool_scale.reshape(1, pw).astype(F32))


def _outproj_kernel(a_ref, p_ref, wa_ref, wp_ref, x_ref, o_ref):
    acc = jnp.dot(a_ref[...], wa_ref[...], preferred_element_type=F32)
    acc = acc + jnp.dot(p_ref[...], wp_ref[...], preferred_element_type=F32)
    o_ref[...] = x_ref[...] + acc


def _outproj(attn, pool, w_out_bf, x2d, tm, tn):
    n, d = x2d.shape
    half = attn.shape[1]
    return pl.pallas_call(
        _outproj_kernel,
        grid=(n // tm, d // tn),
        in_specs=[
            pl.BlockSpec((tm, half), lambda i, j: (i, 0)),
            pl.BlockSpec((tm, half), lambda i, j: (i, 0)),
            pl.BlockSpec((half, tn), lambda i, j: (0, j)),
            pl.BlockSpec((half, tn), lambda i, j: (1, j)),
            pl.BlockSpec((tm, tn), lambda i, j: (i, j)),
        ],
        out_specs=pl.BlockSpec((tm, tn), lambda i, j: (i, j)),
        out_shape=jax.ShapeDtypeStruct((n, d), F32),
        compiler_params=_cparams(("arbitrary", "arbitrary")),
        name="outproj",
    )(attn, pool, w_out_bf, w_out_bf, x2d)


def _route_rows(lg):
    lane = lax.broadcasted_iota(jnp.int32, lg.shape, 1).astype(F32)
    ninf = -jnp.inf
    nolane = float(LANES)
    rmax = lambda t: jnp.max(t, axis=-1, keepdims=True)
    rmin = lambda t: jnp.min(t, axis=-1, keepdims=True)
    is_g = lane < N_GROUPS
    gmax = rmax(jnp.where(is_g, lg, ninf))
    gsel = rmin(jnp.where(is_g & (lg == gmax), lane, nolane))
    p_g = 1.0 / jnp.sum(jnp.where(is_g, jnp.exp(lg - gmax), 0.0), axis=-1, keepdims=True)
    lo = N_GROUPS + EXPERTS_PER_GROUP * gsel
    in_e = (lane >= lo) & (lane < lo + EXPERTS_PER_GROUP)
    v1 = rmax(jnp.where(in_e, lg, ninf))
    i1 = rmin(jnp.where(in_e & (lg == v1), lane, nolane))
    in_e2 = in_e & (lane != i1)
    v2 = rmax(jnp.where(in_e2, lg, ninf))
    i2 = rmin(jnp.where(in_e2 & (lg == v2), lane, nolane))
    t = jnp.exp(v2 - v1)
    w1 = p_g / (1.0 + t)
    w2 = w1 * t
    out = jnp.where(lane == 0, i1 - N_GROUPS, 0.0)
    out = jnp.where(lane == 1, i2 - N_GROUPS, out)
    out = jnp.where(lane == 2, w1, out)
    return jnp.where(lane == 3, w2, out)


def _router_kernel(ha_ref, hb_ref, nw_ref, wr_ref, hn_ref, rt_ref, *, na_tiles):
    i = pl.program_id(0)

    def body(h_ref):
        h = h_ref[...]
        ms = jnp.mean(h * h, axis=-1, keepdims=True)
        hn = h * lax.rsqrt(ms + NORM_EPS) * nw_ref[...]
        hn_ref[...] = hn
        lg = jnp.dot(hn.astype(BF16), wr_ref[...], preferred_element_type=F32)
        rt_ref[...] = _route_rows(lg)

    @pl.when(i < na_tiles)
    def _():
        body(ha_ref)

    @pl.when(i >= na_tiles)
    def _():
        body(hb_ref)


def _router(h1a, h1b, norm_w, wr_bf, tm):
    (n_a, d), n_b = h1a.shape, h1b.shape[0]
    ta, tb = n_a // tm, n_b // tm
    n = n_a + n_b
    return pl.pallas_call(
        functools.partial(_router_kernel, na_tiles=ta),
        grid=(ta + tb,),
        in_specs=[
            pl.BlockSpec((tm, d), lambda i: (jnp.minimum(i, ta - 1), 0)),
            pl.BlockSpec((tm, d), lambda i: (jnp.maximum(i - ta, 0), 0)),
            pl.BlockSpec((1, d), lambda i: (0, 0)),
            pl.BlockSpec((d, LANES), lambda i: (0, 0)),
        ],
        out_specs=[
            pl.BlockSpec((tm, d), lambda i: (i, 0)),
            pl.BlockSpec((tm, LANES), lambda i: (i, 0)),
        ],
        out_shape=[
            jax.ShapeDtypeStruct((n, d), F32),
            jax.ShapeDtypeStruct((n, LANES), F32),
        ],
        compiler_params=_cparams(("arbitrary",)),
        name="router",
    )(h1a, h1b, norm_w.reshape(1, d), wr_bf)


def _route(rt, bm):
    n = rt.shape[0]
    eid = rt[:, :TOP_K].astype(jnp.int32).reshape(-1)
    wts = rt[:, TOP_K:2 * TOP_K].reshape(-1)
    na = n * TOP_K
    onehot = (eid[:, None] == jnp.arange(N_EXPERTS, dtype=jnp.int32)[None, :]).astype(jnp.int32)
    csum = jnp.cumsum(onehot, axis=0)
    rank = jnp.take_along_axis(csum, eid[:, None], axis=1)[:, 0] - 1
    counts = csum[-1]
    pcounts = ((counts + bm - 1) // bm) * bm
    pends = jnp.cumsum(pcounts)
    pstarts = pends - pcounts
    dest = pstarts[eid] + rank
    n_blocks = -(-na // bm) + N_EXPERTS
    rows = n_blocks * bm
    aidx = jnp.arange(na, dtype=jnp.int32)
    inv = jnp.full((rows,), -1, jnp.int32).at[dest].set(aidx)
    valid = inv >= 0
    src = jnp.maximum(inv, 0)
    tok = src // TOP_K
    slot = jnp.where(valid, (src % TOP_K) * n + src // TOP_K,
                     na + jnp.arange(rows, dtype=jnp.int32) % bm)
    wrow = jnp.where(valid, wts[src], 0.0)
    bstart = jnp.arange(n_blocks, dtype=jnp.int32) * bm
    block_e = jnp.minimum(jnp.sum(bstart[:, None] >= pends[None, :], axis=1),
                          N_EXPERTS - 1).astype(jnp.int32)
    nvalid = jnp.clip(pstarts[block_e] + counts[block_e] - bstart, 0, bm).astype(jnp.int32)
    prev_e = jnp.concatenate([jnp.full((1,), -1, jnp.int32), block_e[:-1]])
    first = ((nvalid > 0) & (block_e != prev_e)).astype(jnp.int32)
    wslot = (jnp.cumsum(first) - 1) % 2
    ids = jnp.arange(N_EXPERTS, dtype=jnp.int32)
    later = (ids[None, :] > ids[:, None]) & (counts[None, :] > 0)
    next_nonempty = jnp.min(jnp.where(later, ids[None, :], N_EXPERTS), axis=1)
    next_e = jnp.where(next_nonempty < N_EXPERTS, next_nonempty, -1)[block_e]
    meta = jnp.stack([block_e, nvalid, first, wslot.astype(jnp.int32), next_e.astype(jnp.int32)])
    return meta, tok, slot, wrow.reshape(rows, 1), n_blocks


def _start_row_gather(tok_ref, x_hbm, buf, sem, blk):
    bm = buf.shape[0]
    for r in range(bm):
        t = tok_ref[blk * bm + r]
        pltpu.make_async_copy(x_hbm.at[pl.ds(t, 1)], buf.at[pl.ds(r, 1)], sem).start()


def _wait_row_gather(x_hbm, buf, sem):
    for r in range(buf.shape[0]):
        pltpu.make_async_copy(x_hbm.at[pl.ds(0, 1)], buf.at[pl.ds(r, 1)], sem).wait()


META_E, META_NV, META_FIRST, META_WSLOT, META_NEXT = range(5)


def _expert_weights(meta_ref, i, pairs, sem):
    first = meta_ref[META_FIRST, i] == 1
    slot = meta_ref[META_WSLOT, i]
    nxt = meta_ref[META_NEXT, i]

    def copies(e, s):
        return [pltpu.make_async_copy(w.at[e], buf.at[s], sem.at[s]) for w, buf in pairs]

    @pl.when(i == 0)
    def _():
        for c in copies(meta_ref[META_E, 0], 0):
            c.start(priority=1)

    @pl.when(first)
    def _():
        for c in copies(0, slot):
            c.wait()

    @pl.when(first & (nxt >= 0))
    def _():
        for c in copies(nxt, 1 - slot):
            c.start(priority=1)

    return slot


def _gateup_kernel(meta_ref, tok_ref, x_hbm, wg_hbm, wu_hbm, h_ref,
                   xa, xb, wg_buf, wu_buf, sem, wsem, *, nb):
    i = pl.program_id(0)
    nv = meta_ref[META_NV, i]
    nv_prev = meta_ref[META_NV, jnp.maximum(i - 1, 0)]
    bufs = ((xa, sem.at[0]), (xb, sem.at[1]))
    ws = _expert_weights(meta_ref, i, [(wg_hbm, wg_buf), (wu_hbm, wu_buf)], wsem)

    @pl.when(i == 0)
    def _():
        _start_row_gather(tok_ref, x_hbm, xa, sem.at[0], 0)

    def step(cur, nxt):
        _wait_row_gather(x_hbm, *cur)
        _start_row_gather(tok_ref, x_hbm, *nxt, jnp.minimum(i + 1, nb - 1))
        x = cur[0][...].astype(BF16)
        g = jnp.dot(x, wg_buf[ws], preferred_element_type=F32)
        u = jnp.dot(x, wu_buf[ws], preferred_element_type=F32)
        h_ref[...] = (g * (1.0 / (1.0 + jnp.exp(-g))) * u).astype(BF16)

    for p in (0, 1):
        @pl.when((i % 2 == p) & (nv > 0))
        def _(p=p):
            step(bufs[p], bufs[1 - p])

        @pl.when((i % 2 == p) & (nv == 0) & (nv_prev > 0))
        def _(p=p):
            _wait_row_gather(x_hbm, *bufs[p])

    @pl.when((i == nb - 1) & (nv > 0))
    def _():
        _wait_row_gather(x_hbm, *bufs[nb % 2])

    @pl.when(nv == 0)
    def _():
        h_ref[...] = jnp.zeros_like(h_ref)


def _gateup(meta, tok, hn2, wg_bf, wu_bf, n_blocks, bm):
    n, d = hn2.shape
    f = wg_bf.shape[2]
    anyspec = pl.BlockSpec(memory_space=pl.ANY)
    grid_spec = pltpu.PrefetchScalarGridSpec(
        num_scalar_prefetch=2,
        grid=(n_blocks,),
        in_specs=[anyspec, anyspec, anyspec],
        out_specs=pl.BlockSpec((bm, f), lambda i, mt, tk: (i, 0)),
        scratch_shapes=[pltpu.VMEM((bm, d), F32), pltpu.VMEM((bm, d), F32),
                        pltpu.VMEM((2, d, f), BF16), pltpu.VMEM((2, d, f), BF16),
                        pltpu.SemaphoreType.DMA((2,)), pltpu.SemaphoreType.DMA((2,))],
    )
    return pl.pallas_call(
        functools.partial(_gateup_kernel, nb=n_blocks),
        grid_spec=grid_spec,
        out_shape=jax.ShapeDtypeStruct((n_blocks * bm, f), BF16),
        compiler_params=_cparams(("arbitrary",)),
        name="gateup",
    )(meta, tok, hn2, wg_bf, wu_bf)


def _start_row_scatter(slot_ref, buf, y_hbm, sem, blk):
    bm = buf.shape[0]
    for r in range(bm):
        s = slot_ref[blk * bm + r]
        pltpu.make_async_copy(buf.at[pl.ds(r, 1)], y_hbm.at[pl.ds(s, 1)], sem).start(
            priority=r % 2)


def _wait_row_scatter(buf, y_hbm, sem):
    for r in range(buf.shape[0]):
        pltpu.make_async_copy(buf.at[pl.ds(r, 1)], y_hbm.at[pl.ds(0, 1)], sem).wait()


def _down_kernel(meta_ref, slot_ref, h_ref, wd_hbm, w_ref, y_hbm, ya, yb, wd_buf, sem, wsem,
                 *, nb):
    i = pl.program_id(0)
    nv = meta_ref[META_NV, i]
    nv_prev = meta_ref[META_NV, jnp.maximum(i - 1, 0)]
    bufs = ((ya, sem.at[0]), (yb, sem.at[1]))
    ws = _expert_weights(meta_ref, i, [(wd_hbm, wd_buf)], wsem)

    def compute(buf):
        y = jnp.dot(h_ref[...], wd_buf[ws], preferred_element_type=F32)
        buf[...] = y * w_ref[...]

    @pl.when(i == 0)
    def _():
        bm, na = yb.shape[0], y_hbm.shape[0] - yb.shape[0]
        yb[...] = jnp.zeros_like(yb)
        spare = pltpu.make_async_copy(yb, y_hbm.at[pl.ds(na, bm)], sem.at[1])
        spare.start()
        spare.wait()
        compute(ya)

    for p in (0, 1):
        cur, oth = bufs[p], bufs[1 - p]

        @pl.when((i % 2 == p) & (i >= 2) & (nv_prev > 0))
        def _(cur=cur):
            _wait_row_scatter(cur[0], y_hbm, cur[1])

        @pl.when((i % 2 == p) & (i >= 1) & (nv > 0))
        def _(cur=cur, oth=oth):
            _start_row_scatter(slot_ref, oth[0], y_hbm, oth[1], i - 1)
            compute(cur[0])

        @pl.when((i % 2 == p) & (nv == 0) & (nv_prev > 0))
        def _(oth=oth):
            _start_row_scatter(slot_ref, oth[0], y_hbm, oth[1], i - 1)
            _wait_row_scatter(oth[0], y_hbm, oth[1])

    @pl.when((i == nb - 1) & (nv > 0))
    def _():
        cur, oth = bufs[(nb - 1) % 2], bufs[nb % 2]
        if nb > 1:
            _wait_row_scatter(oth[0], y_hbm, oth[1])
        _start_row_scatter(slot_ref, cur[0], y_hbm, cur[1], nb - 1)
        _wait_row_scatter(cur[0], y_hbm, cur[1])


def _down(meta, slot, h, wd_bf, wrow, n_assign, n_blocks, bm):
    f, d = wd_bf.shape[1], wd_bf.shape[2]
    grid_spec = pltpu.PrefetchScalarGridSpec(
        num_scalar_prefetch=2,
        grid=(n_blocks,),
        in_specs=[
            pl.BlockSpec((bm, f), lambda i, mt, sl: (i, 0)),
            pl.BlockSpec(memory_space=pl.ANY),
            pl.BlockSpec((bm, 1), lambda i, mt, sl: (i, 0)),
        ],
        out_specs=pl.BlockSpec(memory_space=pl.ANY),
        scratch_shapes=[pltpu.VMEM((bm, d), F32), pltpu.VMEM((bm, d), F32),
                        pltpu.VMEM((2, f, d), BF16),
                        pltpu.SemaphoreType.DMA((2,)), pltpu.SemaphoreType.DMA((2,))],
    )
    return pl.pallas_call(
        functools.partial(_down_kernel, nb=n_blocks),
        grid_spec=grid_spec,
        out_shape=jax.ShapeDtypeStruct((n_assign + bm, d), F32),
        compiler_params=_cparams(("arbitrary",)),
        name="down",
    )(meta, slot, h, wd_bf, wrow)


def _final_kernel(h_ref, y0_ref, y1_ref, nw_ref, o_ref):
    h = h_ref[...] + (y0_ref[...] + y1_ref[...])
    ms = jnp.mean(h * h, axis=-1, keepdims=True)
    o_ref[...] = h * lax.rsqrt(ms + NORM_EPS) * nw_ref[...]


def _final(h1, yc, norm_w, tm, row_off, n_tot):
    n, d = h1.shape
    nt = n // tm
    o0 = row_off // tm
    o1 = (n_tot + row_off) // tm
    return pl.pallas_call(
        _final_kernel,
        grid=(nt,),
        in_specs=[
            pl.BlockSpec((tm, d), lambda i: (i, 0)),
            pl.BlockSpec((tm, d), lambda i: (i + o0, 0)),
            pl.BlockSpec((tm, d), lambda i: (i + o1, 0)),
            pl.BlockSpec((1, d), lambda i: (0, 0)),
        ],
        out_specs=pl.BlockSpec((tm, d), lambda i: (i, 0)),
        out_shape=jax.ShapeDtypeStruct((n, d), F32),
        compiler_params=_cparams(("arbitrary",)),
        name="final",
    )(h1, yc, yc, norm_w.reshape(1, d))


def _tiles(seq, d):
    tm = min(512, seq)
    return dict(tm=tm, tn=min(512, d // 2), tq=min(256, seq), tm_out=tm,
                tn_out=min(1024, d), tm_fin=min(256, seq), bm=min(256, seq))


def _mixer(x, shared, prm):
    b, seq, d = x.shape
    n = b * seq
    t = _tiles(seq, d)
    x2d = x.reshape(n, d)
    qk, v, p = _inproj(x2d, prm["attn_norm_w"], shared["w_in"], shared["rope_real"], seq,
                       t["tm"], t["tn"])
    attn = _attention(shared["lam"], qk, v, shared["qk_meta"], shared["v_meta"],
                      prm["subln_w"], b, seq, t["tq"])
    pool = _pool(p, shared["p_meta"], shared["pool_w"], prm["pool_scale"], seq, t["tm"])
    return _outproj(attn, pool, shared["w_out"], x2d, t["tm_out"], t["tn_out"])


def _moe_and_final(h1a, h1b, shapes, shared, prm):
    seq, d = shapes[0][1], shapes[0][2]
    t = _tiles(seq, d)
    bm = t["bm"]
    n_a, n_b = h1a.shape[0], h1b.shape[0]
    n = n_a + n_b
    hn2, rt = _router(h1a, h1b, prm["ffn_norm_w"], shared["w_router"], t["tm"])
    meta, tok, slot, wrow, n_blocks = _route(rt, bm)
    h = _gateup(meta, tok, hn2, shared["w_gate"], shared["w_up"], n_blocks, bm)
    yc = _down(meta, slot, h, shared["w_down"], wrow, n * TOP_K, n_blocks, bm)
    out_a = _final(h1a, yc, prm["final_norm_w"], t["tm_fin"], 0, n)
    out_b = _final(h1b, yc, prm["final_norm_w"], t["tm_fin"], n_a, n)
    return out_a.reshape(shapes[0]), out_b.reshape(shapes[1])


def kernel(x_prompt, x_sample, meta_tokens, attn_norm_w, w_in, lambda_q1, lambda_k1, lambda_q2, lambda_k2, subln_w, pool_w, pool_scale, w_out, ffn_norm_w, router_group, router_expert, expert_w_gate, expert_w_up, expert_w_down, final_norm_w):
    d = x_prompt.shape[-1]
    seq = x_prompt.shape[1]
    assert x_sample.shape[1] == seq and attn_norm_w.shape[0] == 1
    prm = dict(attn_norm_w=attn_norm_w[0], subln_w=subln_w[0], pool_scale=pool_scale[0],
               ffn_norm_w=ffn_norm_w[0], final_norm_w=final_norm_w)
    wr = jnp.concatenate([router_group[0], router_expert[0]], axis=1)
    wr = jnp.pad(wr, ((0, 0), (0, LANES - wr.shape[1])))
    attn_w = d // 2
    w_in_bf = jnp.concatenate([_head_lane_order(w_in[0][:, :2 * attn_w]),
                               w_in[0][:, 2 * attn_w:]], axis=1).astype(BF16)
    shared = dict(
        w_in=w_in_bf, w_out=w_out[0].astype(BF16), pool_w=pool_w[0].astype(BF16),
        w_router=wr.astype(BF16), w_gate=expert_w_gate[0].astype(BF16),
        w_up=expert_w_up[0].astype(BF16), w_down=expert_w_down[0].astype(BF16),
        lam=_lam(lambda_q1[0], lambda_k1[0], lambda_q2[0], lambda_k2[0]),
        rope_real=_rope_tables(jnp.arange(N_META, N_META + seq, dtype=F32)),
    )
    meta_pad = jnp.pad(meta_tokens.astype(F32), ((0, META_ROWS - N_META), (0, 0)))
    t = _tiles(seq, d)
    qk_m, v_m, p_m = _inproj(meta_pad, prm["attn_norm_w"], shared["w_in"],
                             _rope_tables(jnp.arange(META_ROWS, dtype=F32)), META_ROWS,
                             META_ROWS, t["tn"])
    shared.update(qk_meta=qk_m, v_meta=v_m, p_meta=p_m)
    h1a = _mixer(x_prompt, shared, prm)
    h1b = _mixer(x_sample, shared, prm)
    return _moe_and_final(h1a, h1b, (x_prompt.shape, x_sample.shape), shared, prm)
```

```python
import functools
import math

import jax
import jax.numpy as jnp
from jax import lax
from jax.experimental import pallas as pl
from jax.experimental.pallas import tpu as pltpu

F32 = jnp.float32
BF16 = jnp.bfloat16

N_META = 16
HEAD_DIM = 64
V_DIM = 2 * HEAD_DIM
ROT_DIM = HEAD_DIM // 4
ROPE_THETA = 500000.0
ATTN_SCALE = HEAD_DIM ** -0.5
SUBLN_EPS = 1e-5
NORM_EPS = 1e-6
POOL_WINDOWS = (2, 4, 8, 16)
N_GROUPS = 4
EXPERTS_PER_GROUP = 8
N_EXPERTS = N_GROUPS * EXPERTS_PER_GROUP
TOP_K = 2
LAMBDA_INIT = 0.8 - 0.6 * math.exp(-0.3 * 0)

LANES = 128
SUBLANES = 8
META_ROWS = 128
HALO = 8
MASK_VALUE = -1e30

VMEM_LIMIT = 56 * 1024 * 1024


def _cparams(sem):
    return pltpu.CompilerParams(dimension_semantics=sem, vmem_limit_bytes=VMEM_LIMIT)


def _inproj_kernel(x_ref, nw_ref, w_ref, c_ref, s_ref,
                   qk_ref, v_ref, p_ref, hn_ref, *, nqk, nv):
    j = pl.program_id(1)

    @pl.when(j == 0)
    def _():
        x = x_ref[...]
        ms = jnp.mean(x * x, axis=-1, keepdims=True)
        hn_ref[...] = (x * lax.rsqrt(ms + NORM_EPS) * nw_ref[...]).astype(BF16)

    acc = jnp.dot(hn_ref[...], w_ref[...], preferred_element_type=F32)

    @pl.when(j < nqk)
    def _():
        scale = jnp.where(j < nqk // 2, ATTN_SCALE, 1.0).astype(F32)
        c = c_ref[...] * scale
        s = s_ref[...] * scale
        for t in range(acc.shape[1] // LANES):
            blk = acc[:, t * LANES:(t + 1) * LANES]
            r = blk * c + pltpu.roll(blk, LANES // 2, 1) * s
            qk_ref[:, t * LANES:(t + 1) * LANES] = r.astype(BF16)

    @pl.when((j >= nqk) & (j < nqk + nv))
    def _():
        lane = lax.broadcasted_iota(jnp.int32, (acc.shape[0], LANES), 1)
        ones_col = jnp.where(lane == 0, 1.0, 0.0).astype(BF16)
        for t in range(acc.shape[1] // LANES):
            v_ref[:, 2 * t * LANES:(2 * t + 1) * LANES] = acc[:, t * LANES:(t + 1) * LANES].astype(BF16)
            v_ref[:, (2 * t + 1) * LANES:(2 * t + 2) * LANES] = ones_col

    @pl.when(j >= nqk + nv)
    def _():
        p_ref[...] = acc


_HALF = ROT_DIM // 2
HEAD_LANE_SRC = (list(range(0, _HALF)) + list(range(HEAD_DIM, HEAD_DIM + _HALF))
                 + list(range(ROT_DIM, HEAD_DIM))
                 + list(range(_HALF, ROT_DIM)) + list(range(HEAD_DIM + _HALF, HEAD_DIM + ROT_DIM))
                 + list(range(HEAD_DIM + ROT_DIM, 2 * HEAD_DIM)))
COMPONENT0_LANES = tuple(i for i, src in enumerate(HEAD_LANE_SRC) if src < HEAD_DIM)


def _head_lane_order(w):
    lead = w.shape[:-1]
    w3 = w.reshape(lead + (w.shape[-1] // V_DIM, V_DIM))
    runs, start = [], 0
    for i in range(1, V_DIM + 1):
        if i == V_DIM or HEAD_LANE_SRC[i] != HEAD_LANE_SRC[i - 1] + 1:
            runs.append(w3[..., HEAD_LANE_SRC[start]:HEAD_LANE_SRC[i - 1] + 1])
            start = i
    return jnp.concatenate(runs, axis=-1).reshape(w.shape)


def _rope_tables(pos):
    inv_freq = ROPE_THETA ** (-jnp.arange(0, ROT_DIM, 2, dtype=F32) / ROT_DIM)
    ang = pos[:, None] * inv_freq[None, :]
    cos, sin = jnp.cos(ang), jnp.sin(ang)
    r = pos.shape[0]
    ones = jnp.ones((r, HEAD_DIM - ROT_DIM), F32)
    zeros = jnp.zeros((r, HEAD_DIM - ROT_DIM), F32)
    c = jnp.concatenate([cos, cos, ones, cos, cos, ones], axis=1)
    s = jnp.concatenate([-sin, -sin, zeros, sin, sin, zeros], axis=1)
    return c, s


def _inproj(x2d, norm_w, w_in_bf, tables, rows_per_seq, tm, tn):
    n, d = x2d.shape
    d_in = w_in_bf.shape[1]
    attn_w = d // 2
    nqk = 2 * attn_w // tn
    nv = attn_w // tn
    npool = (d_in - 3 * attn_w) // tn
    tps = rows_per_seq // tm
    c, s = tables
    tab_spec = pl.BlockSpec((tm, LANES), lambda i, j: (i % tps, 0))
    kern = functools.partial(_inproj_kernel, nqk=nqk, nv=nv)
    return pl.pallas_call(
        kern,
        grid=(n // tm, d_in // tn),
        in_specs=[
            pl.BlockSpec((tm, d), lambda i, j: (i, 0)),
            pl.BlockSpec((1, d), lambda i, j: (0, 0)),
            pl.BlockSpec((d, tn), lambda i, j: (0, j)),
            tab_spec, tab_spec,
        ],
        out_specs=[
            pl.BlockSpec((tm, tn), lambda i, j: (i, jnp.minimum(j, nqk - 1))),
            pl.BlockSpec((tm, 2 * tn), lambda i, j: (i, jnp.clip(j - nqk, 0, nv - 1))),
            pl.BlockSpec((tm, tn), lambda i, j: (i, jnp.clip(j - nqk - nv, 0, npool - 1))),
        ],
        out_shape=[
            jax.ShapeDtypeStruct((n, 2 * attn_w), BF16),
            jax.ShapeDtypeStruct((n, 2 * attn_w), BF16),
            jax.ShapeDtypeStruct((n, d_in - 3 * attn_w), F32),
        ],
        scratch_shapes=[pltpu.VMEM((tm, d), BF16)],
        compiler_params=_cparams(("arbitrary", "arbitrary")),
        name="inproj",
    )(x2d, norm_w.reshape(1, d), w_in_bf, c, s)


def _lam_kernel(q1_ref, k1_ref, q2_ref, k2_ref, o_ref):
    s1 = jnp.sum(q1_ref[...] * k1_ref[...], axis=-1, keepdims=True)
    s2 = jnp.sum(q2_ref[...] * k2_ref[...], axis=-1, keepdims=True)
    o_ref[...] = jnp.exp(s1) - jnp.exp(s2) + LAMBDA_INIT


def _lam(lq1, lk1, lq2, lk2):
    r = lambda t: t.reshape(1, -1).astype(F32)
    return pl.pallas_call(
        _lam_kernel,
        out_shape=jax.ShapeDtypeStruct((1, 1), F32),
        name="lam",
    )(r(lq1), r(lk1), r(lq2), r(lk2))


def _attn_step(lam_ref, q_ref, k_ref, km_ref, v_ref, vm_ref, sw_ref, o_ref,
               s_w, sm_w, m_w, s_r, sm_r, m_r, oe_w, oe_r):
    tq = q_ref.shape[0]

    oe = oe_r[...]
    r = 1.0 / oe[:, V_DIM:V_DIM + 1]
    on = oe[:, :V_DIM] * r
    o = on[:tq] - lam_ref[...] * on[tq:]
    ms = jnp.mean(o * o, axis=-1, keepdims=True)
    y = o * lax.rsqrt(ms + SUBLN_EPS) * sw_ref[...]
    o_ref[...] = (y * (1.0 - LAMBDA_INIT)).astype(BF16)

    m_p = m_r[:, :1]
    e = jnp.exp(s_r[...] - m_p).astype(BF16)
    em = jnp.exp(sm_r[...] - m_p).astype(BF16)
    oe_w[...] = (jnp.dot(e, v_ref[...], preferred_element_type=F32)
                 + jnp.dot(em, vm_ref[...], preferred_element_type=F32))

    q = q_ref[...]
    lane = lax.broadcasted_iota(jnp.int32, q.shape, 1)
    zero = jnp.zeros_like(q)
    assert COMPONENT0_LANES == tuple(range(0, 8)) + tuple(range(16, 72))
    comp0 = (lane < 8) | ((lane >= 16) & (lane < 72))
    qs = jnp.concatenate([jnp.where(comp0, q, zero),
                          jnp.where(comp0, zero, q)], axis=0)
    dn = (((1,), (1,)), ((), ()))
    s = lax.dot_general(qs, k_ref[...], dn, preferred_element_type=F32)
    sm = lax.dot_general(qs, km_ref[...], dn, preferred_element_type=F32)
    col = lax.broadcasted_iota(jnp.int32, sm.shape, 1)
    sm = jnp.where(col < N_META, sm, MASK_VALUE)
    m = jnp.maximum(jnp.max(s, axis=-1, keepdims=True), jnp.max(sm, axis=-1, keepdims=True))
    s_w[...] = s
    sm_w[...] = sm
    m_w[...] = jnp.broadcast_to(m, m_w.shape)


def _attn_kernel(*refs, n_cast, cast_blocks):
    lam_ref, q_ref, k_ref, km_ref, v_ref, vm_ref, sw_ref = refs[:7]
    cast_in = refs[7:7 + n_cast]
    o_ref = refs[7 + n_cast]
    cast_out = refs[8 + n_cast:8 + 2 * n_cast]
    s_a, s_b, sm_a, sm_b, m_a, m_b, oe_a, oe_b = refs[8 + 2 * n_cast:]
    t = pl.program_id(0)
    args = (lam_ref, q_ref, k_ref, km_ref, v_ref, vm_ref, sw_ref, o_ref)

    for j in range(n_cast):
        @pl.when((t >= j * cast_blocks) & (t < (j + 1) * cast_blocks))
        def _(j=j):
            cast_out[j][...] = cast_in[j][...].astype(BF16)

    @pl.when(t == 0)
    def _():
        s_b[...] = jnp.zeros_like(s_b)
        sm_b[...] = jnp.zeros_like(sm_b)
        m_b[...] = jnp.zeros_like(m_b)
        oe_a[...] = jnp.ones_like(oe_a)

    @pl.when(t % 2 == 0)
    def _():
        _attn_step(*args, s_a, sm_a, m_a, s_b, sm_b, m_b, oe_b, oe_a)

    @pl.when(t % 2 == 1)
    def _():
        _attn_step(*args, s_b, sm_b, m_b, s_a, sm_a, m_a, oe_a, oe_b)


def _attention(lam, qk, vx, qk_meta, vx_meta, subln_w, batch, seq, tq, cast_ws=()):
    n, two_w = qk.shape
    n_heads = two_w // 2 // V_DIM
    nqt = seq // tq
    units = batch * n_heads * nqt
    vw = 2 * V_DIM

    def unit(u):
        bh = u // nqt
        return bh // n_heads, bh % n_heads, u % nqt

    def cur(t):
        return unit(jnp.minimum(t, units - 1))

    def prv(t):
        return unit(jnp.clip(t - 1, 0, units - 1))

    def prv2(t):
        return unit(jnp.maximum(t - 2, 0))

    def q_map(t):
        b, h, i = cur(t)
        return (b * nqt + i, h)

    def k_map(t):
        b, h, _ = cur(t)
        return (b, n_heads + h)

    def km_map(t):
        _, h, _ = cur(t)
        return (0, n_heads + h)

    def v_map(t):
        b, h, _ = prv(t)
        return (b, h)

    def vm_map(t):
        _, h, _ = prv(t)
        return (0, h)

    def o_map(t):
        b, h, i = prv2(t)
        return (b * nqt + i, h)

    steps = units + 2
    n_cast = len(cast_ws)
    cast_blocks = 0
    cast_2d, cast_specs, cast_shapes = [], [], []
    if n_cast:
        cast_blocks = 1 << ((steps // n_cast).bit_length() - 1)
        for j, w in enumerate(cast_ws):
            w2 = w.reshape(-1, w.shape[-1])
            rows = w2.shape[0] // cast_blocks
            assert rows * cast_blocks == w2.shape[0] and rows % 16 == 0, (w.shape, cast_blocks)
            spec = pl.BlockSpec(
                (rows, w2.shape[1]),
                lambda t, j=j: (jnp.clip(t - j * cast_blocks, 0, cast_blocks - 1), 0))
            cast_2d.append(w2)
            cast_specs.append(spec)
            cast_shapes.append(jax.ShapeDtypeStruct(w2.shape, BF16))

    outs = pl.pallas_call(
        functools.partial(_attn_kernel, n_cast=n_cast, cast_blocks=cast_blocks),
        grid=(steps,),
        in_specs=[
            pl.BlockSpec((1, 1), lambda t: (0, 0)),
            pl.BlockSpec((tq, V_DIM), q_map),
            pl.BlockSpec((seq, V_DIM), k_map),
            pl.BlockSpec((META_ROWS, V_DIM), km_map),
            pl.BlockSpec((seq, vw), v_map),
            pl.BlockSpec((META_ROWS, vw), vm_map),
            pl.BlockSpec((1, V_DIM), lambda t: (0, 0)),
        ] + cast_specs,
        out_specs=[pl.BlockSpec((tq, V_DIM), o_map)] + cast_specs,
        out_shape=[jax.ShapeDtypeStruct((n, two_w // 2), BF16)] + cast_shapes,
        scratch_shapes=(
            [pltpu.VMEM((2 * tq, seq), F32)] * 2
            + [pltpu.VMEM((2 * tq, META_ROWS), F32)] * 2
            + [pltpu.VMEM((2 * tq, LANES), F32)] * 2
            + [pltpu.VMEM((2 * tq, vw), F32)] * 2
        ),
        compiler_params=_cparams(("arbitrary",)),
        name="attn",
    )(lam, qk, qk, qk_meta, vx, vx_meta, subln_w.reshape(1, V_DIM).astype(F32), *cast_2d)
    return outs[0], [o.reshape(w.shape) for o, w in zip(outs[1:], cast_ws)]


def _pool_kernel(p_ref, prev_ref, next_ref, pm_ref, w_ref, sc_ref, o_ref, *, tps, total_t):
    i = pl.program_id(0)
    tm = p_ref.shape[0]
    ti = i % tps
    p = p_ref[...]
    prev = jnp.where(ti == 0, pm_ref[...], prev_ref[...])
    nxt = jnp.where(ti == tps - 1, 0.0, next_ref[...])
    ext = jnp.concatenate([prev, p, nxt], axis=0)
    rows = tm + 2 * HALO
    shift = lambda t, k: pltpu.roll(t, rows - k, 0)
    t_pos = N_META + ti * tm + lax.broadcasted_iota(jnp.int32, (tm, 1), 0)
    pch = p.shape[1] // len(POOL_WINDOWS)
    for g, w in enumerate(POOL_WINDOWS):
        sl = slice(g * pch, (g + 1) * pch)
        acc = ext[:, sl]
        span = 1
        while span < w:
            acc = acc + shift(acc, span)
            span *= 2
        off = HALO - w // 2
        win = (shift(acc, off) if off else acc)[:tm]
        cnt = jnp.minimum(w, total_t - t_pos + w // 2).astype(F32)
        z = (win / cnt - p[:, sl]).astype(BF16)
        y = jnp.dot(z, w_ref[g], preferred_element_type=F32)
        o_ref[:, sl] = (y * sc_ref[:, sl]).astype(BF16)


def _pool(p, p_meta, pool_w_bf, pool_scale, seq, tm):
    n, pw = p.shape
    tps = seq // tm
    hb = tm // HALO
    nb = n // HALO
    kern = functools.partial(_pool_kernel, tps=tps, total_t=seq + N_META)
    return pl.pallas_call(
        kern,
        grid=(n // tm,),
        in_specs=[
            pl.BlockSpec((tm, pw), lambda i: (i, 0)),
            pl.BlockSpec((HALO, pw), lambda i: (jnp.maximum(i * hb - 1, 0), 0)),
            pl.BlockSpec((HALO, pw), lambda i: (jnp.minimum((i + 1) * hb, nb - 1), 0)),
            pl.BlockSpec((HALO, pw), lambda i: (N_META // HALO - 1, 0)),
            pl.BlockSpec(pool_w_bf.shape, lambda i: (0, 0, 0)),
            pl.BlockSpec((1, pw), lambda i: (0, 0)),
        ],
        out_specs=pl.BlockSpec((tm, pw), lambda i: (i, 0)),
        out_shape=jax.ShapeDtypeStruct((n, pw), BF16),
        compiler_params=_cparams(("arbitrary",)),
        name="pool",
    )(p, p, p, p_meta, pool_w_bf, pool_scale.reshape(1, pw).astype(F32))


def _outproj_kernel(a_ref, p_ref, wa_ref, wp_ref, x_ref, o_ref):
    acc = jnp.dot(a_ref[...], wa_ref[...], preferred_element_type=F32)
    acc = acc + jnp.dot(p_ref[...], wp_ref[...], preferred_element_type=F32)
    o_ref[...] = x_ref[...] + acc


def _outproj(attn, pool, w_out_bf, x2d, tm, tn):
    n, d = x2d.shape
    half = attn.shape[1]
    return pl.pallas_call(
        _outproj_kernel,
        grid=(n // tm, d // tn),
        in_specs=[
            pl.BlockSpec((tm, half), lambda i, j: (i, 0)),
            pl.BlockSpec((tm, half), lambda i, j: (i, 0)),
            pl.BlockSpec((half, tn), lambda i, j: (0, j)),
            pl.BlockSpec((half, tn), lambda i, j: (1, j)),
            pl.BlockSpec((tm, tn), lambda i, j: (i, j)),
        ],
        out_specs=pl.BlockSpec((tm, tn), lambda i, j: (i, j)),
        out_shape=jax.ShapeDtypeStruct((n, d), F32),
        compiler_params=_cparams(("arbitrary", "arbitrary")),
        name="outproj",
    )(attn, pool, w_out_bf, w_out_bf, x2d)


def _route_rows(lg):
    lane = lax.broadcasted_iota(jnp.int32, lg.shape, 1).astype(F32)
    ninf = -jnp.inf
    nolane = float(LANES)
    rmax = lambda t: jnp.max(t, axis=-1, keepdims=True)
    rmin = lambda t: jnp.min(t, axis=-1, keepdims=True)
    is_g = lane < N_GROUPS
    gmax = rmax(jnp.where(is_g, lg, ninf))
    gsel = rmin(jnp.where(is_g & (lg == gmax), lane, nolane))
    p_g = 1.0 / jnp.sum(jnp.where(is_g, jnp.exp(lg - gmax), 0.0), axis=-1, keepdims=True)
    lo = N_GROUPS + EXPERTS_PER_GROUP * gsel
    in_e = (lane >= lo) & (lane < lo + EXPERTS_PER_GROUP)
    v1 = rmax(jnp.where(in_e, lg, ninf))
    i1 = rmin(jnp.where(in_e & (lg == v1), lane, nolane))
    in_e2 = in_e & (lane != i1)
    v2 = rmax(jnp.where(in_e2, lg, ninf))
    i2 = rmin(jnp.where(in_e2 & (lg == v2), lane, nolane))
    t = jnp.exp(v2 - v1)
    w1 = p_g / (1.0 + t)
    w2 = w1 * t
    out = jnp.where(lane == 0, i1 - N_GROUPS, 0.0)
    out = jnp.where(lane == 1, i2 - N_GROUPS, out)
    out = jnp.where(lane == 2, w1, out)
    return jnp.where(lane == 3, w2, out)


def _store_token_major(ref, x):
    rows, k = x.shape[0], x.shape[1] // LANES
    for c in range(k):
        ref[pl.ds(c, rows, stride=k), :] = x[:, c * LANES:(c + 1) * LANES]


def _load_token_major(ref, rows):
    k = ref.shape[0] // rows
    return [ref[pl.ds(c, rows, stride=k), :] for c in range(k)]


def _store_packed_rows(ref, x_bf):
    half = x_bf.shape[1] // 2
    lo = pltpu.bitcast(x_bf[:, :half].astype(F32), jnp.uint32)
    hi = pltpu.bitcast(x_bf[:, half:].astype(F32), jnp.uint32)
    _store_token_major(ref, (lo >> 16) | (hi & jnp.uint32(0xFFFF0000)))


def _load_packed_rows(ref, rows):
    words = _load_token_major(ref, rows)
    lo = [pltpu.bitcast(w << 16, F32).astype(BF16) for w in words]
    hi = [pltpu.bitcast(w & jnp.uint32(0xFFFF0000), F32).astype(BF16) for w in words]
    return jnp.concatenate(lo + hi, axis=1)


def _router_kernel(ha_ref, hb_ref, nw_ref, wr_ref, hn_ref, rt_ref, *, na_tiles):
    i = pl.program_id(0)

    def body(h_ref):
        h = h_ref[...]
        ms = jnp.mean(h * h, axis=-1, keepdims=True)
        hn = (h * lax.rsqrt(ms + NORM_EPS) * nw_ref[...]).astype(BF16)
        _store_packed_rows(hn_ref, hn)
        lg = jnp.dot(hn, wr_ref[...], preferred_element_type=F32)
        rt_ref[...] = _route_rows(lg)

    @pl.when(i < na_tiles)
    def _():
        body(ha_ref)

    @pl.when(i >= na_tiles)
    def _():
        body(hb_ref)


def _router(h1a, h1b, norm_w, wr_bf, tm):
    (n_a, d), n_b = h1a.shape, h1b.shape[0]
    ta, tb = n_a // tm, n_b // tm
    n = n_a + n_b
    kw = d // 2 // LANES
    return pl.pallas_call(
        functools.partial(_router_kernel, na_tiles=ta),
        grid=(ta + tb,),
        in_specs=[
            pl.BlockSpec((tm, d), lambda i: (jnp.minimum(i, ta - 1), 0)),
            pl.BlockSpec((tm, d), lambda i: (jnp.maximum(i - ta, 0), 0)),
            pl.BlockSpec((1, d), lambda i: (0, 0)),
            pl.BlockSpec((d, LANES), lambda i: (0, 0)),
        ],
        out_specs=[
            pl.BlockSpec((tm * kw, LANES), lambda i: (i, 0)),
            pl.BlockSpec((tm, LANES), lambda i: (i, 0)),
        ],
        out_shape=[
            jax.ShapeDtypeStruct((n * kw, LANES), jnp.uint32),
            jax.ShapeDtypeStruct((n, LANES), F32),
        ],
        compiler_params=_cparams(("arbitrary",)),
        name="router",
    )(h1a, h1b, norm_w.reshape(1, d), wr_bf)


def _route(rt, bm):
    n = rt.shape[0]
    eid = rt[:, :TOP_K].astype(jnp.int32).reshape(-1)
    wts = rt[:, TOP_K:2 * TOP_K].reshape(-1)
    na = n * TOP_K
    onehot = (eid[:, None] == jnp.arange(N_EXPERTS, dtype=jnp.int32)[None, :]).astype(jnp.int32)
    csum = jnp.cumsum(onehot, axis=0)
    rank = jnp.take_along_axis(csum, eid[:, None], axis=1)[:, 0] - 1
    counts = csum[-1]
    pcounts = ((counts + bm - 1) // bm) * bm
    pends = jnp.cumsum(pcounts)
    pstarts = pends - pcounts
    dest = pstarts[eid] + rank
    n_blocks = -(-na // bm) + N_EXPERTS
    rows = n_blocks * bm
    aidx = jnp.arange(na, dtype=jnp.int32)
    inv = jnp.full((rows,), -1, jnp.int32).at[dest].set(aidx)
    valid = inv >= 0
    src = jnp.maximum(inv, 0)
    tok = src // TOP_K
    slot = jnp.where(valid, (src % TOP_K) * n + src // TOP_K,
                     na + jnp.arange(rows, dtype=jnp.int32) % bm)
    wrow = jnp.where(valid, wts[src], 0.0)
    bstart = jnp.arange(n_blocks, dtype=jnp.int32) * bm
    block_e = jnp.minimum(jnp.sum(bstart[:, None] >= pends[None, :], axis=1),
                          N_EXPERTS - 1).astype(jnp.int32)
    nvalid = jnp.clip(pstarts[block_e] + counts[block_e] - bstart, 0, bm).astype(jnp.int32)
    prev_e = jnp.concatenate([jnp.full((1,), -1, jnp.int32), block_e[:-1]])
    first = ((nvalid > 0) & (block_e != prev_e)).astype(jnp.int32)
    wslot = (jnp.cumsum(first) - 1) % 2
    ids = jnp.arange(N_EXPERTS, dtype=jnp.int32)
    later = (ids[None, :] > ids[:, None]) & (counts[None, :] > 0)
    next_nonempty = jnp.min(jnp.where(later, ids[None, :], N_EXPERTS), axis=1)
    next_e = jnp.where(next_nonempty < N_EXPERTS, next_nonempty, -1)[block_e]
    meta = jnp.stack([block_e, nvalid, first, wslot.astype(jnp.int32), next_e.astype(jnp.int32)])
    return meta, tok, slot, wrow.reshape(rows, 1), n_blocks


def _start_row_gather(tok_ref, x_hbm, buf, sem, blk, bm):
    k = buf.shape[0] // bm
    for r in range(bm):
        t = pl.multiple_of(tok_ref[blk * bm + r] * k, k)
        pltpu.make_async_copy(x_hbm.at[pl.ds(t, k)], buf.at[pl.ds(r * k, k)], sem).start(
            priority=r % 2)


def _wait_row_gather(x_hbm, buf, sem, bm):
    k = buf.shape[0] // bm
    for r in range(bm):
        pltpu.make_async_copy(x_hbm.at[pl.ds(0, k)], buf.at[pl.ds(r * k, k)], sem).wait()


META_E, META_NV, META_FIRST, META_WSLOT, META_NEXT = range(5)


def _expert_weights(meta_ref, i, pairs, sem):
    first = meta_ref[META_FIRST, i] == 1
    slot = meta_ref[META_WSLOT, i]
    nxt = meta_ref[META_NEXT, i]

    def copies(e, s):
        return [pltpu.make_async_copy(w.at[e], buf.at[s], sem.at[s]) for w, buf in pairs]

    @pl.when(i == 0)
    def _():
        for c in copies(meta_ref[META_E, 0], 0):
            c.start(priority=1)

    @pl.when(first)
    def _():
        for c in copies(0, slot):
            c.wait()

    @pl.when(first & (nxt >= 0))
    def _():
        for c in copies(nxt, 1 - slot):
            c.start(priority=1)

    return slot


RING = 3


def _gateup_kernel(meta_ref, tok_ref, x_hbm, wg_hbm, wu_hbm, h_ref,
                   xbuf, wg_buf, wu_buf, sem, wsem, *, nb):
    i = pl.program_id(0)
    bm = h_ref.shape[0]
    nv = meta_ref[META_NV, i]
    nv_prev = meta_ref[META_NV, jnp.maximum(i - 1, 0)]
    ws = _expert_weights(meta_ref, i, [(wg_hbm, wg_buf), (wu_hbm, wu_buf)], wsem)

    def start(blk):
        s = blk % RING
        _start_row_gather(tok_ref, x_hbm, xbuf.at[s], sem.at[s], jnp.minimum(blk, nb - 1), bm)

    def wait(blk):
        s = blk % RING
        _wait_row_gather(x_hbm, xbuf.at[s], sem.at[s], bm)

    @pl.when(i == 0)
    def _():
        start(0)
        start(1)

    @pl.when(nv > 0)
    def _():
        wait(i)
        start(i + 2)
        x = _load_packed_rows(xbuf.at[i % RING], bm)
        g = jnp.dot(x, wg_buf[ws], preferred_element_type=F32)
        u = jnp.dot(x, wu_buf[ws], preferred_element_type=F32)
        h_ref[...] = (g * (1.0 / (1.0 + jnp.exp(-g))) * u).astype(BF16)

    @pl.when((nv == 0) & (nv_prev > 0))
    def _():
        wait(i)
        wait(i + 1)

    @pl.when((i == nb - 1) & (nv > 0))
    def _():
        wait(nb)
        wait(nb + 1)

    @pl.when(nv == 0)
    def _():
        h_ref[...] = jnp.zeros_like(h_ref)


def _gateup(meta, tok, hn2p, wg_bf, wu_bf, n_blocks, bm):
    d, f = wg_bf.shape[1], wg_bf.shape[2]
    kw = d // 2 // LANES
    anyspec = pl.BlockSpec(memory_space=pl.ANY)
    grid_spec = pltpu.PrefetchScalarGridSpec(
        num_scalar_prefetch=2,
        grid=(n_blocks,),
        in_specs=[anyspec, anyspec, anyspec],
        out_specs=pl.BlockSpec((bm, f), lambda i, mt, tk: (i, 0)),
        scratch_shapes=[pltpu.VMEM((RING, bm * kw, LANES), jnp.uint32),
                        pltpu.VMEM((2, d, f), BF16), pltpu.VMEM((2, d, f), BF16),
                        pltpu.SemaphoreType.DMA((RING,)), pltpu.SemaphoreType.DMA((2,))],
    )
    return pl.pallas_call(
        functools.partial(_gateup_kernel, nb=n_blocks),
        grid_spec=grid_spec,
        out_shape=jax.ShapeDtypeStruct((n_blocks * bm, f), BF16),
        compiler_params=_cparams(("arbitrary",)),
        name="gateup",
    )(meta, tok, hn2p, wg_bf, wu_bf)


def _start_row_scatter(slot_ref, buf, y_hbm, sem, blk, bm):
    k = buf.shape[0] // bm
    for r in range(bm):
        s = pl.multiple_of(slot_ref[blk * bm + r] * k, k)
        pltpu.make_async_copy(buf.at[pl.ds(r * k, k)], y_hbm.at[pl.ds(s, k)], sem).start(
            priority=r % 2)


def _wait_row_scatter(buf, y_hbm, sem, bm):
    k = buf.shape[0] // bm
    for r in range(bm):
        pltpu.make_async_copy(buf.at[pl.ds(r * k, k)], y_hbm.at[pl.ds(0, k)], sem).wait()


def _down_kernel(meta_ref, slot_ref, h_ref, wd_hbm, w_ref, y_hbm, ybuf, wd_buf, sem, wsem,
                 *, nb):
    i = pl.program_id(0)
    bm = h_ref.shape[0]
    nv = meta_ref[META_NV, i]
    nv_prev = meta_ref[META_NV, jnp.maximum(i - 1, 0)]
    first_empty = (nv == 0) & (nv_prev > 0)
    ws = _expert_weights(meta_ref, i, [(wd_hbm, wd_buf)], wsem)

    def compute():
        y = jnp.dot(h_ref[...], wd_buf[ws], preferred_element_type=F32)
        _store_token_major(ybuf.at[i % RING], y * w_ref[...])

    def start(blk):
        s = blk % RING
        _start_row_scatter(slot_ref, ybuf.at[s], y_hbm, sem.at[s], blk, bm)

    def wait(blk):
        s = blk % RING
        _wait_row_scatter(ybuf.at[s], y_hbm, sem.at[s], bm)

    @pl.when(i == 0)
    def _():
        rows = ybuf.shape[1]
        ybuf[1] = jnp.zeros(ybuf.shape[1:], F32)
        spare = pltpu.make_async_copy(ybuf.at[1], y_hbm.at[pl.ds(y_hbm.shape[0] - rows, rows)],
                                      sem.at[1])
        spare.start()
        spare.wait()
        compute()

    @pl.when((i >= 3) & ((nv > 0) | first_empty))
    def _():
        wait(i - 3)

    @pl.when((i >= 1) & (nv > 0))
    def _():
        start(i - 1)
        compute()

    @pl.when((i >= 2) & first_empty)
    def _():
        wait(i - 2)

    @pl.when(first_empty)
    def _():
        start(i - 1)
        wait(i - 1)

    @pl.when((i == nb - 1) & (nv > 0))
    def _():
        if nb > 2:
            wait(nb - 3)
        if nb > 1:
            wait(nb - 2)
        start(nb - 1)
        wait(nb - 1)


def _down(meta, slot, h, wd_bf, wrow, n_assign, n_blocks, bm):
    f, d = wd_bf.shape[1], wd_bf.shape[2]
    kd = d // LANES
    grid_spec = pltpu.PrefetchScalarGridSpec(
        num_scalar_prefetch=2,
        grid=(n_blocks,),
        in_specs=[
            pl.BlockSpec((bm, f), lambda i, mt, sl: (i, 0)),
            pl.BlockSpec(memory_space=pl.ANY),
            pl.BlockSpec((bm, 1), lambda i, mt, sl: (i, 0)),
        ],
        out_specs=pl.BlockSpec(memory_space=pl.ANY),
        scratch_shapes=[pltpu.VMEM((RING, bm * kd, LANES), F32),
                        pltpu.VMEM((2, f, d), BF16),
                        pltpu.SemaphoreType.DMA((RING,)), pltpu.SemaphoreType.DMA((2,))],
    )
    return pl.pallas_call(
        functools.partial(_down_kernel, nb=n_blocks),
        grid_spec=grid_spec,
        out_shape=jax.ShapeDtypeStruct(((n_assign + bm) * kd, LANES), F32),
        compiler_params=_cparams(("arbitrary",)),
        name="down",
    )(meta, slot, h, wd_bf, wrow)


def _final_kernel(h_ref, y0_ref, y1_ref, nw_ref, o_ref):
    tm = h_ref.shape[0]
    y = [a + b for a, b in zip(_load_token_major(y0_ref, tm), _load_token_major(y1_ref, tm))]
    h = h_ref[...] + jnp.concatenate(y, axis=1)
    ms = jnp.mean(h * h, axis=-1, keepdims=True)
    o_ref[...] = h * lax.rsqrt(ms + NORM_EPS) * nw_ref[...]


def _final(h1, yc, norm_w, tm, row_off, n_tot):
    n, d = h1.shape
    nt = n // tm
    o0 = row_off // tm
    o1 = (n_tot + row_off) // tm
    kd = d // LANES
    return pl.pallas_call(
        _final_kernel,
        grid=(nt,),
        in_specs=[
            pl.BlockSpec((tm, d), lambda i: (i, 0)),
            pl.BlockSpec((tm * kd, LANES), lambda i: (i + o0, 0)),
            pl.BlockSpec((tm * kd, LANES), lambda i: (i + o1, 0)),
            pl.BlockSpec((1, d), lambda i: (0, 0)),
        ],
        out_specs=pl.BlockSpec((tm, d), lambda i: (i, 0)),
        out_shape=jax.ShapeDtypeStruct((n, d), F32),
        compiler_params=_cparams(("arbitrary",)),
        name="final",
    )(h1, yc, yc, norm_w.reshape(1, d))


def _tiles(seq, d):
    tm = min(512, seq)
    return dict(tm=tm, tn=min(512, d // 2), tq=min(256, seq), tm_out=tm,
                tn_out=min(1024, d), tm_fin=min(256, seq), bm=min(256, seq))


def _mixer(x, shared, prm, cast_ws):
    b, seq, d = x.shape
    n = b * seq
    t = _tiles(seq, d)
    x2d = x.reshape(n, d)
    qk, v, p = _inproj(x2d, prm["attn_norm_w"], shared["w_in"], shared["rope_real"], seq,
                       t["tm"], t["tn"])
    attn, cast = _attention(shared["lam"], qk, v, shared["qk_meta"], shared["v_meta"],
                            prm["subln_w"], b, seq, t["tq"], cast_ws)
    pool = _pool(p, shared["p_meta"], shared["pool_w"], prm["pool_scale"], seq, t["tm"])
    return _outproj(attn, pool, shared["w_out"], x2d, t["tm_out"], t["tn_out"]), cast


def _moe_and_final(h1a, h1b, shapes, shared, prm):
    seq, d = shapes[0][1], shapes[0][2]
    t = _tiles(seq, d)
    bm = t["bm"]
    n_a, n_b = h1a.shape[0], h1b.shape[0]
    n = n_a + n_b
    hn2, rt = _router(h1a, h1b, prm["ffn_norm_w"], shared["w_router"], t["tm"])
    meta, tok, slot, wrow, n_blocks = _route(rt, bm)
    h = _gateup(meta, tok, hn2, shared["w_gate"], shared["w_up"], n_blocks, bm)
    yc = _down(meta, slot, h, shared["w_down"], wrow, n * TOP_K, n_blocks, bm)
    out_a = _final(h1a, yc, prm["final_norm_w"], t["tm_fin"], 0, n)
    out_b = _final(h1b, yc, prm["final_norm_w"], t["tm_fin"], n_a, n)
    return out_a.reshape(shapes[0]), out_b.reshape(shapes[1])


def kernel(x_prompt, x_sample, meta_tokens, attn_norm_w, w_in, lambda_q1, lambda_k1, lambda_q2, lambda_k2, subln_w, pool_w, pool_scale, w_out, ffn_norm_w, router_group, router_expert, expert_w_gate, expert_w_up, expert_w_down, final_norm_w):
    d = x_prompt.shape[-1]
    seq = x_prompt.shape[1]
    assert x_sample.shape[1] == seq and attn_norm_w.shape[0] == 1
    prm = dict(attn_norm_w=attn_norm_w[0], subln_w=subln_w[0], pool_scale=pool_scale[0],
               ffn_norm_w=ffn_norm_w[0], final_norm_w=final_norm_w)
    wr = jnp.concatenate([router_group[0], router_expert[0]], axis=1)
    wr = jnp.pad(wr, ((0, 0), (0, LANES - wr.shape[1])))
    attn_w = d // 2
    w_in_bf = w_in[0].astype(BF16)
    w_in_bf = jnp.concatenate([_head_lane_order(w_in_bf[:, :2 * attn_w]),
                               w_in_bf[:, 2 * attn_w:]], axis=1)
    shared = dict(
        w_in=w_in_bf, w_out=w_out[0].astype(BF16), pool_w=pool_w[0].astype(BF16),
        w_router=wr.astype(BF16),
        lam=_lam(lambda_q1[0], lambda_k1[0], lambda_q2[0], lambda_k2[0]),
        rope_real=_rope_tables(jnp.arange(N_META, N_META + seq, dtype=F32)),
    )
    meta_pad = jnp.pad(meta_tokens.astype(F32), ((0, META_ROWS - N_META), (0, 0)))
    t = _tiles(seq, d)
    qk_m, v_m, p_m = _inproj(meta_pad, prm["attn_norm_w"], shared["w_in"],
                             _rope_tables(jnp.arange(META_ROWS, dtype=F32)), META_ROWS,
                             META_ROWS, t["tn"])
    shared.update(qk_meta=qk_m, v_meta=v_m, p_meta=p_m)
    h1a, (w_down_bf,) = _mixer(x_prompt, shared, prm, (expert_w_down[0],))
    h1b, (w_gate_bf, w_up_bf) = _mixer(x_sample, shared, prm,
                                       (expert_w_gate[0], expert_w_up[0]))
    shared.update(w_gate=w_gate_bf, w_up=w_up_bf, w_down=w_down_bf)
    return _moe_and_final(h1a, h1b, (x_prompt.shape, x_sample.shape), shared, prm)
```

```python
import functools
import math

import jax
import jax.numpy as jnp
from jax import lax
from jax.experimental import pallas as pl
from jax.experimental.pallas import tpu as pltpu

F32 = jnp.float32
BF16 = jnp.bfloat16

N_META = 16
HEAD_DIM = 64
V_DIM = 2 * HEAD_DIM
ROT_DIM = HEAD_DIM // 4
ROPE_THETA = 500000.0
ATTN_SCALE = HEAD_DIM ** -0.5
SUBLN_EPS = 1e-5
NORM_EPS = 1e-6
POOL_WINDOWS = (2, 4, 8, 16)
N_GROUPS = 4
EXPERTS_PER_GROUP = 8
N_EXPERTS = N_GROUPS * EXPERTS_PER_GROUP
TOP_K = 2
LAMBDA_INIT = 0.8 - 0.6 * math.exp(-0.3 * 0)

LANES = 128
SUBLANES = 8
META_ROWS = 128
HALO = 8
MASK_VALUE = -1e30

VMEM_LIMIT = 56 * 1024 * 1024


def _cparams(sem):
    return pltpu.CompilerParams(dimension_semantics=sem, vmem_limit_bytes=VMEM_LIMIT)


def _inproj_kernel(x_ref, nw_ref, w_ref, c_ref, s_ref,
                   qk_ref, v_ref, p_ref, hn_ref, *, nqk, nv):
    j = pl.program_id(1)

    @pl.when(j == 0)
    def _():
        x = x_ref[...]
        ms = jnp.mean(x * x, axis=-1, keepdims=True)
        hn_ref[...] = (x * lax.rsqrt(ms + NORM_EPS) * nw_ref[...]).astype(BF16)

    acc = jnp.dot(hn_ref[...], w_ref[...], preferred_element_type=F32)

    @pl.when(j < nqk)
    def _():
        scale = jnp.where(j < nqk // 2, ATTN_SCALE, 1.0).astype(F32)
        c = c_ref[...] * scale
        s = s_ref[...] * scale
        for t in range(acc.shape[1] // LANES):
            blk = acc[:, t * LANES:(t + 1) * LANES]
            r = blk * c + pltpu.roll(blk, LANES // 2, 1) * s
            qk_ref[:, t * LANES:(t + 1) * LANES] = r.astype(BF16)

    @pl.when((j >= nqk) & (j < nqk + nv))
    def _():
        lane = lax.broadcasted_iota(jnp.int32, (acc.shape[0], LANES), 1)
        ones_col = jnp.where(lane == 0, 1.0, 0.0).astype(BF16)
        for t in range(acc.shape[1] // LANES):
            v_ref[:, 2 * t * LANES:(2 * t + 1) * LANES] = acc[:, t * LANES:(t + 1) * LANES].astype(BF16)
            v_ref[:, (2 * t + 1) * LANES:(2 * t + 2) * LANES] = ones_col

    @pl.when(j >= nqk + nv)
    def _():
        p_ref[...] = acc


_HALF = ROT_DIM // 2
HEAD_LANE_SRC = (list(range(0, _HALF)) + list(range(HEAD_DIM, HEAD_DIM + _HALF))
                 + list(range(ROT_DIM, HEAD_DIM))
                 + list(range(_HALF, ROT_DIM)) + list(range(HEAD_DIM + _HALF, HEAD_DIM + ROT_DIM))
                 + list(range(HEAD_DIM + ROT_DIM, 2 * HEAD_DIM)))
COMPONENT0_LANES = tuple(i for i, src in enumerate(HEAD_LANE_SRC) if src < HEAD_DIM)


def _prep_w_in_kernel(w_ref, perm_ref, o_ref, *, nqk):
    j = pl.program_id(1)

    @pl.when(j < nqk)
    def _():
        for t in range(w_ref.shape[1] // LANES):
            sl = slice(t * LANES, (t + 1) * LANES)
            o_ref[:, sl] = jnp.dot(w_ref[:, sl].astype(BF16), perm_ref[...],
                                   preferred_element_type=F32).astype(BF16)

    @pl.when(j >= nqk)
    def _():
        o_ref[...] = w_ref[...].astype(BF16)


def _prep_w_in(w_in, tn):
    d, d_in = w_in.shape
    nqk = d // tn
    perm = jnp.zeros((V_DIM, V_DIM), F32).at[jnp.array(HEAD_LANE_SRC), jnp.arange(V_DIM)].set(1.0)
    tr = min(1024, d)
    return pl.pallas_call(
        functools.partial(_prep_w_in_kernel, nqk=nqk),
        grid=(d // tr, d_in // tn),
        in_specs=[pl.BlockSpec((tr, tn), lambda i, j: (i, j)),
                  pl.BlockSpec((V_DIM, V_DIM), lambda i, j: (0, 0))],
        out_specs=pl.BlockSpec((tr, tn), lambda i, j: (i, j)),
        out_shape=jax.ShapeDtypeStruct((d, d_in), BF16),
        compiler_params=_cparams(("arbitrary", "arbitrary")),
        name="prep_w_in",
    )(w_in, perm.astype(BF16))


def _rope_tables(pos):
    inv_freq = ROPE_THETA ** (-jnp.arange(0, ROT_DIM, 2, dtype=F32) / ROT_DIM)
    ang = pos[:, None] * inv_freq[None, :]
    cos, sin = jnp.cos(ang), jnp.sin(ang)
    r = pos.shape[0]
    ones = jnp.ones((r, HEAD_DIM - ROT_DIM), F32)
    zeros = jnp.zeros((r, HEAD_DIM - ROT_DIM), F32)
    c = jnp.concatenate([cos, cos, ones, cos, cos, ones], axis=1)
    s = jnp.concatenate([-sin, -sin, zeros, sin, sin, zeros], axis=1)
    return c, s


def _inproj(x2d, norm_w, w_in_bf, tables, rows_per_seq, tm, tn):
    n, d = x2d.shape
    d_in = w_in_bf.shape[1]
    attn_w = d // 2
    nqk = 2 * attn_w // tn
    nv = attn_w // tn
    npool = (d_in - 3 * attn_w) // tn
    tps = rows_per_seq // tm
    c, s = tables
    tab_spec = pl.BlockSpec((tm, LANES), lambda i, j: (i % tps, 0))
    kern = functools.partial(_inproj_kernel, nqk=nqk, nv=nv)
    return pl.pallas_call(
        kern,
        grid=(n // tm, d_in // tn),
        in_specs=[
            pl.BlockSpec((tm, d), lambda i, j: (i, 0)),
            pl.BlockSpec((1, d), lambda i, j: (0, 0)),
            pl.BlockSpec((d, tn), lambda i, j: (0, j)),
            tab_spec, tab_spec,
        ],
        out_specs=[
            pl.BlockSpec((tm, tn), lambda i, j: (i, jnp.minimum(j, nqk - 1))),
            pl.BlockSpec((tm, 2 * tn), lambda i, j: (i, jnp.clip(j - nqk, 0, nv - 1))),
            pl.BlockSpec((tm, tn), lambda i, j: (i, jnp.clip(j - nqk - nv, 0, npool - 1))),
        ],
        out_shape=[
            jax.ShapeDtypeStruct((n, 2 * attn_w), BF16),
            jax.ShapeDtypeStruct((n, 2 * attn_w), BF16),
            jax.ShapeDtypeStruct((n, d_in - 3 * attn_w), F32),
        ],
        scratch_shapes=[pltpu.VMEM((tm, d), BF16)],
        compiler_params=_cparams(("arbitrary", "arbitrary")),
        name="inproj",
    )(x2d, norm_w.reshape(1, d), w_in_bf, c, s)


def _lam_kernel(q1_ref, k1_ref, q2_ref, k2_ref, o_ref):
    s1 = jnp.sum(q1_ref[...] * k1_ref[...], axis=-1, keepdims=True)
    s2 = jnp.sum(q2_ref[...] * k2_ref[...], axis=-1, keepdims=True)
    o_ref[...] = jnp.exp(s1) - jnp.exp(s2) + LAMBDA_INIT


def _lam(lq1, lk1, lq2, lk2):
    r = lambda t: t.reshape(1, -1).astype(F32)
    return pl.pallas_call(
        _lam_kernel,
        out_shape=jax.ShapeDtypeStruct((1, 1), F32),
        name="lam",
    )(r(lq1), r(lk1), r(lq2), r(lk2))


def _attn_step(lam_ref, q_ref, k_ref, km_ref, v_ref, vm_ref, sw_ref, o_ref,
               s_w, sm_w, m_w, s_r, sm_r, m_r, oe_w, oe_r):
    tq = q_ref.shape[0]

    oe = oe_r[...]
    r = 1.0 / oe[:, V_DIM:V_DIM + 1]
    on = oe[:, :V_DIM] * r
    o = on[:tq] - lam_ref[...] * on[tq:]
    ms = jnp.mean(o * o, axis=-1, keepdims=True)
    y = o * lax.rsqrt(ms + SUBLN_EPS) * sw_ref[...]
    o_ref[...] = (y * (1.0 - LAMBDA_INIT)).astype(BF16)

    m_p = m_r[:, :1]
    e = jnp.exp(s_r[...] - m_p).astype(BF16)
    em = jnp.exp(sm_r[...] - m_p).astype(BF16)
    oe_w[...] = (jnp.dot(e, v_ref[...], preferred_element_type=F32)
                 + jnp.dot(em, vm_ref[...], preferred_element_type=F32))

    q = q_ref[...]
    lane = lax.broadcasted_iota(jnp.int32, q.shape, 1)
    zero = jnp.zeros_like(q)
    assert COMPONENT0_LANES == tuple(range(0, 8)) + tuple(range(16, 72))
    comp0 = (lane < 8) | ((lane >= 16) & (lane < 72))
    qs = jnp.concatenate([jnp.where(comp0, q, zero),
                          jnp.where(comp0, zero, q)], axis=0)
    dn = (((1,), (1,)), ((), ()))
    s = lax.dot_general(qs, k_ref[...], dn, preferred_element_type=F32)
    sm = lax.dot_general(qs, km_ref[...], dn, preferred_element_type=F32)
    col = lax.broadcasted_iota(jnp.int32, sm.shape, 1)
    sm = jnp.where(col < N_META, sm, MASK_VALUE)
    m = jnp.maximum(jnp.max(s, axis=-1, keepdims=True), jnp.max(sm, axis=-1, keepdims=True))
    s_w[...] = s
    sm_w[...] = sm
    m_w[...] = jnp.broadcast_to(m, m_w.shape)


def _attn_kernel(*refs, n_cast, cast_blocks):
    lam_ref, q_ref, k_ref, km_ref, v_ref, vm_ref, sw_ref = refs[:7]
    cast_in = refs[7:7 + n_cast]
    o_ref = refs[7 + n_cast]
    cast_out = refs[8 + n_cast:8 + 2 * n_cast]
    s_a, s_b, sm_a, sm_b, m_a, m_b, oe_a, oe_b = refs[8 + 2 * n_cast:]
    t = pl.program_id(0)
    args = (lam_ref, q_ref, k_ref, km_ref, v_ref, vm_ref, sw_ref, o_ref)

    for j in range(n_cast):
        @pl.when((t >= j * cast_blocks) & (t < (j + 1) * cast_blocks))
        def _(j=j):
            cast_out[j][...] = cast_in[j][...].astype(BF16)

    @pl.when(t == 0)
    def _():
        s_b[...] = jnp.zeros_like(s_b)
        sm_b[...] = jnp.zeros_like(sm_b)
        m_b[...] = jnp.zeros_like(m_b)
        oe_a[...] = jnp.ones_like(oe_a)

    @pl.when(t % 2 == 0)
    def _():
        _attn_step(*args, s_a, sm_a, m_a, s_b, sm_b, m_b, oe_b, oe_a)

    @pl.when(t % 2 == 1)
    def _():
        _attn_step(*args, s_b, sm_b, m_b, s_a, sm_a, m_a, oe_a, oe_b)


def _attention(lam, qk, vx, qk_meta, vx_meta, subln_w, batch, seq, tq, cast_ws=()):
    n, two_w = qk.shape
    n_heads = two_w // 2 // V_DIM
    nqt = seq // tq
    units = batch * n_heads * nqt
    vw = 2 * V_DIM

    def unit(u):
        bh = u // nqt
        return bh // n_heads, bh % n_heads, u % nqt

    def cur(t):
        return unit(jnp.minimum(t, units - 1))

    def prv(t):
        return unit(jnp.clip(t - 1, 0, units - 1))

    def prv2(t):
        return unit(jnp.maximum(t - 2, 0))

    def q_map(t):
        b, h, i = cur(t)
        return (b * nqt + i, h)

    def k_map(t):
        b, h, _ = cur(t)
        return (b, n_heads + h)

    def km_map(t):
        _, h, _ = cur(t)
        return (0, n_heads + h)

    def v_map(t):
        b, h, _ = prv(t)
        return (b, h)

    def vm_map(t):
        _, h, _ = prv(t)
        return (0, h)

    def o_map(t):
        b, h, i = prv2(t)
        return (b * nqt + i, h)

    steps = units + 2
    n_cast = len(cast_ws)
    cast_blocks = 0
    cast_2d, cast_specs, cast_shapes = [], [], []
    if n_cast:
        cast_blocks = 1 << ((steps // n_cast).bit_length() - 1)
        for j, w in enumerate(cast_ws):
            w2 = w.reshape(-1, w.shape[-1])
            rows = w2.shape[0] // cast_blocks
            assert rows * cast_blocks == w2.shape[0] and rows % 16 == 0, (w.shape, cast_blocks)
            spec = pl.BlockSpec(
                (rows, w2.shape[1]),
                lambda t, j=j: (jnp.clip(t - j * cast_blocks, 0, cast_blocks - 1), 0))
            cast_2d.append(w2)
            cast_specs.append(spec)
            cast_shapes.append(jax.ShapeDtypeStruct(w2.shape, BF16))

    outs = pl.pallas_call(
        functools.partial(_attn_kernel, n_cast=n_cast, cast_blocks=cast_blocks),
        grid=(steps,),
        in_specs=[
            pl.BlockSpec((1, 1), lambda t: (0, 0)),
            pl.BlockSpec((tq, V_DIM), q_map),
            pl.BlockSpec((seq, V_DIM), k_map),
            pl.BlockSpec((META_ROWS, V_DIM), km_map),
            pl.BlockSpec((seq, vw), v_map),
            pl.BlockSpec((META_ROWS, vw), vm_map),
            pl.BlockSpec((1, V_DIM), lambda t: (0, 0)),
        ] + cast_specs,
        out_specs=[pl.BlockSpec((tq, V_DIM), o_map)] + cast_specs,
        out_shape=[jax.ShapeDtypeStruct((n, two_w // 2), BF16)] + cast_shapes,
        scratch_shapes=(
            [pltpu.VMEM((2 * tq, seq), F32)] * 2
            + [pltpu.VMEM((2 * tq, META_ROWS), F32)] * 2
            + [pltpu.VMEM((2 * tq, LANES), F32)] * 2
            + [pltpu.VMEM((2 * tq, vw), F32)] * 2
        ),
        compiler_params=_cparams(("arbitrary",)),
        name="attn",
    )(lam, qk, qk, qk_meta, vx, vx_meta, subln_w.reshape(1, V_DIM).astype(F32), *cast_2d)
    return outs[0], [o.reshape(w.shape) for o, w in zip(outs[1:], cast_ws)]


def _pool_kernel(p_ref, prev_ref, next_ref, pm_ref, w_ref, sc_ref, o_ref, *, tps, total_t):
    i = pl.program_id(0)
    tm = p_ref.shape[0]
    ti = i % tps
    p = p_ref[...]
    prev = jnp.where(ti == 0, pm_ref[...], prev_ref[...])
    nxt = jnp.where(ti == tps - 1, 0.0, next_ref[...])
    ext = jnp.concatenate([prev, p, nxt], axis=0)
    rows = tm + 2 * HALO
    shift = lambda t, k: pltpu.roll(t, rows - k, 0)
    t_pos = N_META + ti * tm + lax.broadcasted_iota(jnp.int32, (tm, 1), 0)
    pch = p.shape[1] // len(POOL_WINDOWS)
    for g, w in enumerate(POOL_WINDOWS):
        sl = slice(g * pch, (g + 1) * pch)
        acc = ext[:, sl]
        span = 1
        while span < w:
            acc = acc + shift(acc, span)
            span *= 2
        off = HALO - w // 2
        win = (shift(acc, off) if off else acc)[:tm]
        cnt = jnp.minimum(w, total_t - t_pos + w // 2).astype(F32)
        z = (win / cnt - p[:, sl]).astype(BF16)
        y = jnp.dot(z, w_ref[g], preferred_element_type=F32)
        o_ref[:, sl] = (y * sc_ref[:, sl]).astype(BF16)


def _pool(p, p_meta, pool_w_bf, pool_scale, seq, tm):
    n, pw = p.shape
    tps = seq // tm
    hb = tm // HALO
    nb = n // HALO
    kern = functools.partial(_pool_kernel, tps=tps, total_t=seq + N_META)
    return pl.pallas_call(
        kern,
        grid=(n // tm,),
        in_specs=[
            pl.BlockSpec((tm, pw), lambda i: (i, 0)),
            pl.BlockSpec((HALO, pw), lambda i: (jnp.maximum(i * hb - 1, 0), 0)),
            pl.BlockSpec((HALO, pw), lambda i: (jnp.minimum((i + 1) * hb, nb - 1), 0)),
            pl.BlockSpec((HALO, pw), lambda i: (N_META // HALO - 1, 0)),
            pl.BlockSpec(pool_w_bf.shape, lambda i: (0, 0, 0)),
            pl.BlockSpec((1, pw), lambda i: (0, 0)),
        ],
        out_specs=pl.BlockSpec((tm, pw), lambda i: (i, 0)),
        out_shape=jax.ShapeDtypeStruct((n, pw), BF16),
        compiler_params=_cparams(("arbitrary",)),
        name="pool",
    )(p, p, p, p_meta, pool_w_bf, pool_scale.reshape(1, pw).astype(F32))


def _outproj_kernel(a_ref, p_ref, wa_ref, wp_ref, x_ref, o_ref):
    acc = jnp.dot(a_ref[...], wa_ref[...], preferred_element_type=F32)
    acc = acc + jnp.dot(p_ref[...], wp_ref[...], preferred_element_type=F32)
    o_ref[...] = x_ref[...] + acc


def _outproj(attn, pool, w_out_bf, x2d, tm, tn):
    n, d = x2d.shape
    half = attn.shape[1]
    return pl.pallas_call(
        _outproj_kernel,
        grid=(n // tm, d // tn),
        in_specs=[
            pl.BlockSpec((tm, half), lambda i, j: (i, 0)),
            pl.BlockSpec((tm, half), lambda i, j: (i, 0)),
            pl.BlockSpec((half, tn), lambda i, j: (0, j)),
            pl.BlockSpec((half, tn), lambda i, j: (1, j)),
            pl.BlockSpec((tm, tn), lambda i, j: (i, j)),
        ],
        out_specs=pl.BlockSpec((tm, tn), lambda i, j: (i, j)),
        out_shape=jax.ShapeDtypeStruct((n, d), F32),
        compiler_params=_cparams(("arbitrary", "arbitrary")),
        name="outproj",
    )(attn, pool, w_out_bf, w_out_bf, x2d)


def _route_rows(lg):
    lane = lax.broadcasted_iota(jnp.int32, lg.shape, 1).astype(F32)
    ninf = -jnp.inf
    nolane = float(LANES)
    rmax = lambda t: jnp.max(t, axis=-1, keepdims=True)
    rmin = lambda t: jnp.min(t, axis=-1, keepdims=True)
    is_g = lane < N_GROUPS
    gmax = rmax(jnp.where(is_g, lg, ninf))
    gsel = rmin(jnp.where(is_g & (lg == gmax), lane, nolane))
    p_g = 1.0 / jnp.sum(jnp.where(is_g, jnp.exp(lg - gmax), 0.0), axis=-1, keepdims=True)
    lo = N_GROUPS + EXPERTS_PER_GROUP * gsel
    in_e = (lane >= lo) & (lane < lo + EXPERTS_PER_GROUP)
    v1 = rmax(jnp.where(in_e, lg, ninf))
    i1 = rmin(jnp.where(in_e & (lg == v1), lane, nolane))
    in_e2 = in_e & (lane != i1)
    v2 = rmax(jnp.where(in_e2, lg, ninf))
    i2 = rmin(jnp.where(in_e2 & (lg == v2), lane, nolane))
    t = jnp.exp(v2 - v1)
    w1 = p_g / (1.0 + t)
    w2 = w1 * t
    out = jnp.where(lane == 0, i1 - N_GROUPS, 0.0)
    out = jnp.where(lane == 1, i2 - N_GROUPS, out)
    out = jnp.where(lane == 2, w1, out)
    return jnp.where(lane == 3, w2, out)


def _store_token_major(ref, x):
    rows, k = x.shape[0], x.shape[1] // LANES
    for c in range(k):
        ref[pl.ds(c, rows, stride=k), :] = x[:, c * LANES:(c + 1) * LANES]


def _load_token_major(ref, rows):
    k = ref.shape[0] // rows
    return [ref[pl.ds(c, rows, stride=k), :] for c in range(k)]


def _store_packed_rows(ref, x_bf):
    half = x_bf.shape[1] // 2
    lo = pltpu.bitcast(x_bf[:, :half].astype(F32), jnp.uint32)
    hi = pltpu.bitcast(x_bf[:, half:].astype(F32), jnp.uint32)
    _store_token_major(ref, (lo >> 16) | (hi & jnp.uint32(0xFFFF0000)))


def _load_packed_rows(ref, rows):
    words = _load_token_major(ref, rows)
    lo = [pltpu.bitcast(w << 16, F32).astype(BF16) for w in words]
    hi = [pltpu.bitcast(w & jnp.uint32(0xFFFF0000), F32).astype(BF16) for w in words]
    return jnp.concatenate(lo + hi, axis=1)


def _router_kernel(ha_ref, hb_ref, nw_ref, wr_ref, hn_ref, rt_ref, *, na_tiles):
    i = pl.program_id(0)

    def body(h_ref):
        h = h_ref[...]
        ms = jnp.mean(h * h, axis=-1, keepdims=True)
        hn = (h * lax.rsqrt(ms + NORM_EPS) * nw_ref[...]).astype(BF16)
        _store_packed_rows(hn_ref, hn)
        lg = jnp.dot(hn, wr_ref[...], preferred_element_type=F32)
        rt_ref[...] = _route_rows(lg)

    @pl.when(i < na_tiles)
    def _():
        body(ha_ref)

    @pl.when(i >= na_tiles)
    def _():
        body(hb_ref)


def _router(h1a, h1b, norm_w, wr_bf, tm):
    (n_a, d), n_b = h1a.shape, h1b.shape[0]
    ta, tb = n_a // tm, n_b // tm
    n = n_a + n_b
    kw = d // 2 // LANES
    return pl.pallas_call(
        functools.partial(_router_kernel, na_tiles=ta),
        grid=(ta + tb,),
        in_specs=[
            pl.BlockSpec((tm, d), lambda i: (jnp.minimum(i, ta - 1), 0)),
            pl.BlockSpec((tm, d), lambda i: (jnp.maximum(i - ta, 0), 0)),
            pl.BlockSpec((1, d), lambda i: (0, 0)),
            pl.BlockSpec((d, LANES), lambda i: (0, 0)),
        ],
        out_specs=[
            pl.BlockSpec((tm * kw, LANES), lambda i: (i, 0)),
            pl.BlockSpec((tm, LANES), lambda i: (i, 0)),
        ],
        out_shape=[
            jax.ShapeDtypeStruct((n * kw, LANES), jnp.uint32),
            jax.ShapeDtypeStruct((n, LANES), F32),
        ],
        compiler_params=_cparams(("arbitrary",)),
        name="router",
    )(h1a, h1b, norm_w.reshape(1, d), wr_bf)


def _route(rt, bm):
    n = rt.shape[0]
    eid = rt[:, :TOP_K].astype(jnp.int32).reshape(-1)
    wts = rt[:, TOP_K:2 * TOP_K].reshape(-1)
    na = n * TOP_K
    onehot = (eid[:, None] == jnp.arange(N_EXPERTS, dtype=jnp.int32)[None, :]).astype(jnp.int32)
    csum = jnp.cumsum(onehot, axis=0)
    rank = jnp.take_along_axis(csum, eid[:, None], axis=1)[:, 0] - 1
    counts = csum[-1]
    pcounts = ((counts + bm - 1) // bm) * bm
    pends = jnp.cumsum(pcounts)
    pstarts = pends - pcounts
    dest = pstarts[eid] + rank
    n_blocks = -(-na // bm) + N_EXPERTS
    rows = n_blocks * bm
    aidx = jnp.arange(na, dtype=jnp.int32)
    inv = jnp.full((rows,), -1, jnp.int32).at[dest].set(aidx)
    valid = inv >= 0
    src = jnp.maximum(inv, 0)
    tok = src // TOP_K
    slot = jnp.where(valid, (src % TOP_K) * n + src // TOP_K,
                     na + jnp.arange(rows, dtype=jnp.int32) % bm)
    wrow = jnp.where(valid, wts[src], 0.0)
    bstart = jnp.arange(n_blocks, dtype=jnp.int32) * bm
    block_e = jnp.minimum(jnp.sum(bstart[:, None] >= pends[None, :], axis=1),
                          N_EXPERTS - 1).astype(jnp.int32)
    nvalid = jnp.clip(pstarts[block_e] + counts[block_e] - bstart, 0, bm).astype(jnp.int32)
    prev_e = jnp.concatenate([jnp.full((1,), -1, jnp.int32), block_e[:-1]])
    first = ((nvalid > 0) & (block_e != prev_e)).astype(jnp.int32)
    wslot = (jnp.cumsum(first) - 1) % 2
    ids = jnp.arange(N_EXPERTS, dtype=jnp.int32)
    later = (ids[None, :] > ids[:, None]) & (counts[None, :] > 0)
    next_nonempty = jnp.min(jnp.where(later, ids[None, :], N_EXPERTS), axis=1)
    next_e = jnp.where(next_nonempty < N_EXPERTS, next_nonempty, -1)[block_e]
    meta = jnp.stack([block_e, nvalid, first, wslot.astype(jnp.int32), next_e.astype(jnp.int32)])
    return meta, tok, slot, wrow.reshape(rows, 1), n_blocks


def _start_row_gather(tok_ref, x_hbm, buf, sem, blk, bm):
    k = buf.shape[0] // bm
    for r in range(bm):
        t = pl.multiple_of(tok_ref[blk * bm + r] * k, k)
        pltpu.make_async_copy(x_hbm.at[pl.ds(t, k)], buf.at[pl.ds(r * k, k)], sem).start(
            priority=r % 2)


def _wait_row_gather(x_hbm, buf, sem, bm):
    k = buf.shape[0] // bm
    for r in range(bm):
        pltpu.make_async_copy(x_hbm.at[pl.ds(0, k)], buf.at[pl.ds(r * k, k)], sem).wait()


META_E, META_NV, META_FIRST, META_WSLOT, META_NEXT = range(5)


def _expert_weights(meta_ref, i, pairs, sem):
    first = meta_ref[META_FIRST, i] == 1
    slot = meta_ref[META_WSLOT, i]
    nxt = meta_ref[META_NEXT, i]

    def copies(e, s):
        return [pltpu.make_async_copy(w.at[e], buf.at[s], sem.at[s]) for w, buf in pairs]

    @pl.when(i == 0)
    def _():
        for c in copies(meta_ref[META_E, 0], 0):
            c.start(priority=1)

    @pl.when(first)
    def _():
        for c in copies(0, slot):
            c.wait()

    @pl.when(first & (nxt >= 0))
    def _():
        for c in copies(nxt, 1 - slot):
            c.start(priority=1)

    return slot


RING = 3


def _gateup_kernel(meta_ref, tok_ref, x_hbm, wg_hbm, wu_hbm, h_ref,
                   xbuf, wg_buf, wu_buf, sem, wsem, *, nb):
    i = pl.program_id(0)
    bm = h_ref.shape[0]
    nv = meta_ref[META_NV, i]
    nv_prev = meta_ref[META_NV, jnp.maximum(i - 1, 0)]
    ws = _expert_weights(meta_ref, i, [(wg_hbm, wg_buf), (wu_hbm, wu_buf)], wsem)

    def start(blk):
        s = blk % RING
        _start_row_gather(tok_ref, x_hbm, xbuf.at[s], sem.at[s], jnp.minimum(blk, nb - 1), bm)

    def wait(blk):
        s = blk % RING
        _wait_row_gather(x_hbm, xbuf.at[s], sem.at[s], bm)

    @pl.when(i == 0)
    def _():
        start(0)
        start(1)

    @pl.when(nv > 0)
    def _():
        wait(i)
        start(i + 2)
        x = _load_packed_rows(xbuf.at[i % RING], bm)
        g = jnp.dot(x, wg_buf[ws], preferred_element_type=F32)
        u = jnp.dot(x, wu_buf[ws], preferred_element_type=F32)
        h_ref[...] = (g * (1.0 / (1.0 + jnp.exp(-g))) * u).astype(BF16)

    @pl.when((nv == 0) & (nv_prev > 0))
    def _():
        wait(i)
        wait(i + 1)

    @pl.when((i == nb - 1) & (nv > 0))
    def _():
        wait(nb)
        wait(nb + 1)

    @pl.when(nv == 0)
    def _():
        h_ref[...] = jnp.zeros_like(h_ref)


def _gateup(meta, tok, hn2p, wg_bf, wu_bf, n_blocks, bm):
    d, f = wg_bf.shape[1], wg_bf.shape[2]
    kw = d // 2 // LANES
    anyspec = pl.BlockSpec(memory_space=pl.ANY)
    grid_spec = pltpu.PrefetchScalarGridSpec(
        num_scalar_prefetch=2,
        grid=(n_blocks,),
        in_specs=[anyspec, anyspec, anyspec],
        out_specs=pl.BlockSpec((bm, f), lambda i, mt, tk: (i, 0)),
        scratch_shapes=[pltpu.VMEM((RING, bm * kw, LANES), jnp.uint32),
                        pltpu.VMEM((2, d, f), BF16), pltpu.VMEM((2, d, f), BF16),
                        pltpu.SemaphoreType.DMA((RING,)), pltpu.SemaphoreType.DMA((2,))],
    )
    return pl.pallas_call(
        functools.partial(_gateup_kernel, nb=n_blocks),
        grid_spec=grid_spec,
        out_shape=jax.ShapeDtypeStruct((n_blocks * bm, f), BF16),
        compiler_params=_cparams(("arbitrary",)),
        name="gateup",
    )(meta, tok, hn2p, wg_bf, wu_bf)


def _start_row_scatter(slot_ref, buf, y_hbm, sem, blk, bm):
    k = buf.shape[0] // bm
    for r in range(bm):
        s = pl.multiple_of(slot_ref[blk * bm + r] * k, k)
        pltpu.make_async_copy(buf.at[pl.ds(r * k, k)], y_hbm.at[pl.ds(s, k)], sem).start(
            priority=r % 2)


def _wait_row_scatter(buf, y_hbm, sem, bm):
    k = buf.shape[0] // bm
    for r in range(bm):
        pltpu.make_async_copy(buf.at[pl.ds(r * k, k)], y_hbm.at[pl.ds(0, k)], sem).wait()


def _down_kernel(meta_ref, slot_ref, h_ref, wd_hbm, w_ref, y_hbm, ybuf, wd_buf, sem, wsem,
                 *, nb):
    i = pl.program_id(0)
    bm = h_ref.shape[0]
    nv = meta_ref[META_NV, i]
    nv_prev = meta_ref[META_NV, jnp.maximum(i - 1, 0)]
    first_empty = (nv == 0) & (nv_prev > 0)
    ws = _expert_weights(meta_ref, i, [(wd_hbm, wd_buf)], wsem)

    def compute():
        y = jnp.dot(h_ref[...], wd_buf[ws], preferred_element_type=F32)
        ybuf[i % RING] = y * w_ref[...]

    def start(blk):
        s = blk % RING
        _start_row_scatter(slot_ref, ybuf.at[s], y_hbm, sem.at[s], blk, bm)

    def wait(blk):
        s = blk % RING
        _wait_row_scatter(ybuf.at[s], y_hbm, sem.at[s], bm)

    @pl.when(i == 0)
    def _():
        rows = ybuf.shape[1]
        ybuf[1] = jnp.zeros(ybuf.shape[1:], F32)
        spare = pltpu.make_async_copy(ybuf.at[1], y_hbm.at[pl.ds(y_hbm.shape[0] - rows, rows)],
                                      sem.at[1])
        spare.start()
        spare.wait()
        compute()

    @pl.when((i >= 3) & ((nv > 0) | first_empty))
    def _():
        wait(i - 3)

    @pl.when((i >= 1) & (nv > 0))
    def _():
        start(i - 1)
        compute()

    @pl.when((i >= 2) & first_empty)
    def _():
        wait(i - 2)

    @pl.when(first_empty)
    def _():
        start(i - 1)
        wait(i - 1)

    @pl.when((i == nb - 1) & (nv > 0))
    def _():
        if nb > 2:
            wait(nb - 3)
        if nb > 1:
            wait(nb - 2)
        start(nb - 1)
        wait(nb - 1)


def _down(meta, slot, h, wd_bf, wrow, n_assign, n_blocks, bm):
    f, d = wd_bf.shape[1], wd_bf.shape[2]
    grid_spec = pltpu.PrefetchScalarGridSpec(
        num_scalar_prefetch=2,
        grid=(n_blocks,),
        in_specs=[
            pl.BlockSpec((bm, f), lambda i, mt, sl: (i, 0)),
            pl.BlockSpec(memory_space=pl.ANY),
            pl.BlockSpec((bm, 1), lambda i, mt, sl: (i, 0)),
        ],
        out_specs=pl.BlockSpec(memory_space=pl.ANY),
        scratch_shapes=[pltpu.VMEM((RING, bm, d), F32),
                        pltpu.VMEM((2, f, d), BF16),
                        pltpu.SemaphoreType.DMA((RING,)), pltpu.SemaphoreType.DMA((2,))],
    )
    return pl.pallas_call(
        functools.partial(_down_kernel, nb=n_blocks),
        grid_spec=grid_spec,
        out_shape=jax.ShapeDtypeStruct((n_assign + bm, d), F32),
        compiler_params=_cparams(("arbitrary",)),
        name="down",
    )(meta, slot, h, wd_bf, wrow)


def _final_kernel(h_ref, y0_ref, y1_ref, nw_ref, o_ref):
    h = h_ref[...] + (y0_ref[...] + y1_ref[...])
    ms = jnp.mean(h * h, axis=-1, keepdims=True)
    o_ref[...] = h * lax.rsqrt(ms + NORM_EPS) * nw_ref[...]


def _final(h1, yc, norm_w, tm, row_off, n_tot):
    n, d = h1.shape
    nt = n // tm
    o0 = row_off // tm
    o1 = (n_tot + row_off) // tm
    return pl.pallas_call(
        _final_kernel,
        grid=(nt,),
        in_specs=[
            pl.BlockSpec((tm, d), lambda i: (i, 0)),
            pl.BlockSpec((tm, d), lambda i: (i + o0, 0)),
            pl.BlockSpec((tm, d), lambda i: (i + o1, 0)),
            pl.BlockSpec((1, d), lambda i: (0, 0)),
        ],
        out_specs=pl.BlockSpec((tm, d), lambda i: (i, 0)),
        out_shape=jax.ShapeDtypeStruct((n, d), F32),
        compiler_params=_cparams(("arbitrary",)),
        name="final",
    )(h1, yc, yc, norm_w.reshape(1, d))


def _tiles(seq, d):
    tm = min(512, seq)
    return dict(tm=tm, tn=min(512, d // 2), tq=min(256, seq), tm_out=tm,
                tn_out=min(1024, d), tm_fin=min(256, seq), bm=min(256, seq))


def _mixer(x, shared, prm, cast_ws):
    b, seq, d = x.shape
    n = b * seq
    t = _tiles(seq, d)
    x2d = x.reshape(n, d)
    qk, v, p = _inproj(x2d, prm["attn_norm_w"], shared["w_in"], shared["rope_real"], seq,
                       t["tm"], t["tn"])
    attn, cast = _attention(shared["lam"], qk, v, shared["qk_meta"], shared["v_meta"],
                            prm["subln_w"], b, seq, t["tq"], cast_ws)
    pool = _pool(p, shared["p_meta"], shared["pool_w"], prm["pool_scale"], seq, t["tm"])
    return _outproj(attn, pool, shared["w_out"], x2d, t["tm_out"], t["tn_out"]), cast


def _moe_and_final(h1a, h1b, shapes, shared, prm):
    seq, d = shapes[0][1], shapes[0][2]
    t = _tiles(seq, d)
    bm = t["bm"]
    n_a, n_b = h1a.shape[0], h1b.shape[0]
    n = n_a + n_b
    hn2, rt = _router(h1a, h1b, prm["ffn_norm_w"], shared["w_router"], t["tm"])
    meta, tok, slot, wrow, n_blocks = _route(rt, bm)
    h = _gateup(meta, tok, hn2, shared["w_gate"], shared["w_up"], n_blocks, bm)
    yc = _down(meta, slot, h, shared["w_down"], wrow, n * TOP_K, n_blocks, bm)
    out_a = _final(h1a, yc, prm["final_norm_w"], t["tm_fin"], 0, n)
    out_b = _final(h1b, yc, prm["final_norm_w"], t["tm_fin"], n_a, n)
    return out_a.reshape(shapes[0]), out_b.reshape(shapes[1])


def kernel(x_prompt, x_sample, meta_tokens, attn_norm_w, w_in, lambda_q1, lambda_k1, lambda_q2, lambda_k2, subln_w, pool_w, pool_scale, w_out, ffn_norm_w, router_group, router_expert, expert_w_gate, expert_w_up, expert_w_down, final_norm_w):
    d = x_prompt.shape[-1]
    seq = x_prompt.shape[1]
    assert x_sample.shape[1] == seq and attn_norm_w.shape[0] == 1
    prm = dict(attn_norm_w=attn_norm_w[0], subln_w=subln_w[0], pool_scale=pool_scale[0],
               ffn_norm_w=ffn_norm_w[0], final_norm_w=final_norm_w)
    wr = jnp.concatenate([router_group[0], router_expert[0]], axis=1)
    wr = jnp.pad(wr, ((0, 0), (0, LANES - wr.shape[1])))
    t = _tiles(seq, d)
    shared = dict(
        w_in=_prep_w_in(w_in[0], t["tn"]), w_out=w_out[0].astype(BF16), pool_w=pool_w[0].astype(BF16),
        w_router=wr.astype(BF16),
        lam=_lam(lambda_q1[0], lambda_k1[0], lambda_q2[0], lambda_k2[0]),
        rope_real=_rope_tables(jnp.arange(N_META, N_META + seq, dtype=F32)),
    )
    meta_pad = jnp.pad(meta_tokens.astype(F32), ((0, META_ROWS - N_META), (0, 0)))
    qk_m, v_m, p_m = _inproj(meta_pad, prm["attn_norm_w"], shared["w_in"],
                             _rope_tables(jnp.arange(META_ROWS, dtype=F32)), META_ROWS,
                             META_ROWS, t["tn"])
    shared.update(qk_meta=qk_m, v_meta=v_m, p_meta=p_m)
    h1a, (w_down_bf,) = _mixer(x_prompt, shared, prm, (expert_w_down[0],))
    h1b, (w_gate_bf, w_up_bf) = _mixer(x_sample, shared, prm,
                                       (expert_w_gate[0], expert_w_up[0]))
    shared.update(w_gate=w_gate_bf, w_up=w_up_bf, w_down=w_down_bf)
    return _moe_and_final(h1a, h1b, (x_prompt.shape, x_sample.shape), shared, prm)
```

```python
import functools
import math

import jax
import jax.numpy as jnp
from jax import lax
from jax.experimental import pallas as pl
from jax.experimental.pallas import tpu as pltpu

F32 = jnp.float32
BF16 = jnp.bfloat16

N_META = 16
HEAD_DIM = 64
V_DIM = 2 * HEAD_DIM
ROT_DIM = HEAD_DIM // 4
ROPE_THETA = 500000.0
ATTN_SCALE = HEAD_DIM ** -0.5
SUBLN_EPS = 1e-5
NORM_EPS = 1e-6
POOL_WINDOWS = (2, 4, 8, 16)
N_GROUPS = 4
EXPERTS_PER_GROUP = 8
N_EXPERTS = N_GROUPS * EXPERTS_PER_GROUP
TOP_K = 2
LAMBDA_INIT = 0.8 - 0.6 * math.exp(-0.3 * 0)

LANES = 128
SUBLANES = 8
META_ROWS = 128
HALO = 8
MASK_VALUE = -1e30

VMEM_LIMIT = 56 * 1024 * 1024


def _cparams(sem):
    return pltpu.CompilerParams(dimension_semantics=sem, vmem_limit_bytes=VMEM_LIMIT)


def _inproj_kernel(x_ref, nw_ref, w_ref, c_ref, s_ref,
                   qk_ref, v_ref, p_ref, hn_ref, *, nqk, nv):
    j = pl.program_id(1)

    @pl.when(j == 0)
    def _():
        x = x_ref[...]
        ms = jnp.mean(x * x, axis=-1, keepdims=True)
        hn_ref[...] = (x * lax.rsqrt(ms + NORM_EPS) * nw_ref[...]).astype(BF16)

    acc = jnp.dot(hn_ref[...], w_ref[...], preferred_element_type=F32)

    @pl.when(j < nqk)
    def _():
        scale = jnp.where(j < nqk // 2, ATTN_SCALE, 1.0).astype(F32)
        c = c_ref[...] * scale
        s = s_ref[...] * scale
        for t in range(acc.shape[1] // LANES):
            blk = acc[:, t * LANES:(t + 1) * LANES]
            r = blk * c + pltpu.roll(blk, LANES // 2, 1) * s
            qk_ref[:, t * LANES:(t + 1) * LANES] = r.astype(BF16)

    @pl.when((j >= nqk) & (j < nqk + nv))
    def _():
        lane = lax.broadcasted_iota(jnp.int32, (acc.shape[0], LANES), 1)
        ones_col = jnp.where(lane == 0, 1.0, 0.0).astype(BF16)
        for t in range(acc.shape[1] // LANES):
            v_ref[:, 2 * t * LANES:(2 * t + 1) * LANES] = acc[:, t * LANES:(t + 1) * LANES].astype(BF16)
            v_ref[:, (2 * t + 1) * LANES:(2 * t + 2) * LANES] = ones_col

    @pl.when(j >= nqk + nv)
    def _():
        p_ref[...] = acc


_HALF = ROT_DIM // 2
HEAD_LANE_SRC = (list(range(0, _HALF)) + list(range(HEAD_DIM, HEAD_DIM + _HALF))
                 + list(range(ROT_DIM, HEAD_DIM))
                 + list(range(_HALF, ROT_DIM)) + list(range(HEAD_DIM + _HALF, HEAD_DIM + ROT_DIM))
                 + list(range(HEAD_DIM + ROT_DIM, 2 * HEAD_DIM)))
COMPONENT0_LANES = tuple(i for i, src in enumerate(HEAD_LANE_SRC) if src < HEAD_DIM)


def _prep_w_in_kernel(w_ref, perm_ref, o_ref, *, nqk):
    j = pl.program_id(1)

    @pl.when(j < nqk)
    def _():
        for t in range(w_ref.shape[1] // LANES):
            sl = slice(t * LANES, (t + 1) * LANES)
            o_ref[:, sl] = jnp.dot(w_ref[:, sl].astype(BF16), perm_ref[...],
                                   preferred_element_type=F32).astype(BF16)

    @pl.when(j >= nqk)
    def _():
        o_ref[...] = w_ref[...].astype(BF16)


def _prep_w_in(w_in, tn):
    d, d_in = w_in.shape
    nqk = d // tn
    perm = jnp.zeros((V_DIM, V_DIM), F32).at[jnp.array(HEAD_LANE_SRC), jnp.arange(V_DIM)].set(1.0)
    tr = min(1024, d)
    return pl.pallas_call(
        functools.partial(_prep_w_in_kernel, nqk=nqk),
        grid=(d // tr, d_in // tn),
        in_specs=[pl.BlockSpec((tr, tn), lambda i, j: (i, j)),
                  pl.BlockSpec((V_DIM, V_DIM), lambda i, j: (0, 0))],
        out_specs=pl.BlockSpec((tr, tn), lambda i, j: (i, j)),
        out_shape=jax.ShapeDtypeStruct((d, d_in), BF16),
        compiler_params=_cparams(("arbitrary", "arbitrary")),
        name="prep_w_in",
    )(w_in, perm.astype(BF16))


def _rope_tables(pos):
    inv_freq = ROPE_THETA ** (-jnp.arange(0, ROT_DIM, 2, dtype=F32) / ROT_DIM)
    ang = pos[:, None] * inv_freq[None, :]
    cos, sin = jnp.cos(ang), jnp.sin(ang)
    r = pos.shape[0]
    ones = jnp.ones((r, HEAD_DIM - ROT_DIM), F32)
    zeros = jnp.zeros((r, HEAD_DIM - ROT_DIM), F32)
    c = jnp.concatenate([cos, cos, ones, cos, cos, ones], axis=1)
    s = jnp.concatenate([-sin, -sin, zeros, sin, sin, zeros], axis=1)
    return c, s


def _inproj(x2d, norm_w, w_in_bf, tables, rows_per_seq, tm, tn):
    n, d = x2d.shape
    d_in = w_in_bf.shape[1]
    attn_w = d // 2
    nqk = 2 * attn_w // tn
    nv = attn_w // tn
    npool = (d_in - 3 * attn_w) // tn
    tps = rows_per_seq // tm
    c, s = tables
    tab_spec = pl.BlockSpec((tm, LANES), lambda i, j: (i % tps, 0))
    kern = functools.partial(_inproj_kernel, nqk=nqk, nv=nv)
    return pl.pallas_call(
        kern,
        grid=(n // tm, d_in // tn),
        in_specs=[
            pl.BlockSpec((tm, d), lambda i, j: (i, 0)),
            pl.BlockSpec((1, d), lambda i, j: (0, 0)),
            pl.BlockSpec((d, tn), lambda i, j: (0, j)),
            tab_spec, tab_spec,
        ],
        out_specs=[
            pl.BlockSpec((tm, tn), lambda i, j: (i, jnp.minimum(j, nqk - 1))),
            pl.BlockSpec((tm, 2 * tn), lambda i, j: (i, jnp.clip(j - nqk, 0, nv - 1))),
            pl.BlockSpec((tm, tn), lambda i, j: (i, jnp.clip(j - nqk - nv, 0, npool - 1))),
        ],
        out_shape=[
            jax.ShapeDtypeStruct((n, 2 * attn_w), BF16),
            jax.ShapeDtypeStruct((n, 2 * attn_w), BF16),
            jax.ShapeDtypeStruct((n, d_in - 3 * attn_w), F32),
        ],
        scratch_shapes=[pltpu.VMEM((tm, d), BF16)],
        compiler_params=_cparams(("arbitrary", "arbitrary")),
        name="inproj",
    )(x2d, norm_w.reshape(1, d), w_in_bf, c, s)


def _lam_kernel(q1_ref, k1_ref, q2_ref, k2_ref, o_ref):
    s1 = jnp.sum(q1_ref[...] * k1_ref[...], axis=-1, keepdims=True)
    s2 = jnp.sum(q2_ref[...] * k2_ref[...], axis=-1, keepdims=True)
    o_ref[...] = jnp.exp(s1) - jnp.exp(s2) + LAMBDA_INIT


def _lam(lq1, lk1, lq2, lk2):
    r = lambda t: t.reshape(1, -1).astype(F32)
    return pl.pallas_call(
        _lam_kernel,
        out_shape=jax.ShapeDtypeStruct((1, 1), F32),
        name="lam",
    )(r(lq1), r(lk1), r(lq2), r(lk2))


def _attn_step(refs, P, Q):
    lam_ref, q_ref, k_ref, km_ref, v_ref, vm_ref, sw_ref, o_ref, cast_in, cast_out = refs
    tq = q_ref.shape[0]

    for w_in, w_out in zip(cast_in, cast_out):
        w_out[...] = w_in[...].astype(BF16)

    oe = P["oe"][...]
    r = 1.0 / oe[:, V_DIM:V_DIM + 1]
    on = oe[:, :V_DIM] * r
    o = on[:tq] - lam_ref[...] * on[tq:]
    ms = jnp.mean(o * o, axis=-1, keepdims=True)
    y = o * lax.rsqrt(ms + SUBLN_EPS) * sw_ref[...]
    o_ref[...] = (y * (1.0 - LAMBDA_INIT)).astype(BF16)

    m_p = Q["m"][:, :1]
    e = jnp.exp(Q["s"][...] - m_p).astype(BF16)
    em = jnp.exp(Q["sm"][...] - m_p).astype(BF16)
    Q["oe"][...] = (jnp.dot(e, v_ref[...], preferred_element_type=F32)
                    + jnp.dot(em, vm_ref[...], preferred_element_type=F32))

    q = q_ref[...]
    lane = lax.broadcasted_iota(jnp.int32, q.shape, 1)
    zero = jnp.zeros_like(q)
    assert COMPONENT0_LANES == tuple(range(0, 8)) + tuple(range(16, 72))
    comp0 = (lane < 8) | ((lane >= 16) & (lane < 72))
    qs = jnp.concatenate([jnp.where(comp0, q, zero),
                          jnp.where(comp0, zero, q)], axis=0)
    dn = (((1,), (1,)), ((), ()))
    s = lax.dot_general(qs, k_ref[...], dn, preferred_element_type=F32)
    sm = lax.dot_general(qs, km_ref[...], dn, preferred_element_type=F32)
    col = lax.broadcasted_iota(jnp.int32, sm.shape, 1)
    sm = jnp.where(col < N_META, sm, MASK_VALUE)
    m = jnp.maximum(jnp.max(s, axis=-1, keepdims=True), jnp.max(sm, axis=-1, keepdims=True))
    P["s"][...] = s
    P["sm"][...] = sm
    P["m"][...] = jnp.broadcast_to(m, P["m"].shape)


ATTN_SCRATCH = ("s", "sm", "m", "oe")


def _attn_kernel(*refs, n_cast):
    lam_ref, q_ref, k_ref, km_ref, v_ref, vm_ref, sw_ref = refs[:7]
    cast_in = refs[7:7 + n_cast]
    o_ref = refs[7 + n_cast]
    cast_out = refs[8 + n_cast:8 + 2 * n_cast]
    scratch = refs[8 + 2 * n_cast:]
    A = dict(zip(ATTN_SCRATCH, scratch[0::2]))
    B = dict(zip(ATTN_SCRATCH, scratch[1::2]))
    t = pl.program_id(0)
    args = (lam_ref, q_ref, k_ref, km_ref, v_ref, vm_ref, sw_ref, o_ref, cast_in, cast_out)

    @pl.when(t == 0)
    def _():
        for name in ("s", "sm", "m"):
            B[name][...] = jnp.zeros_like(B[name])
        A["oe"][...] = jnp.ones_like(A["oe"])

    @pl.when(t % 2 == 0)
    def _():
        _attn_step(args, A, B)

    @pl.when(t % 2 == 1)
    def _():
        _attn_step(args, B, A)


def _attention(lam, qk, vx, qk_meta, vx_meta, subln_w, batch, seq, tq, cast_ws=()):
    n, two_w = qk.shape
    n_heads = two_w // 2 // V_DIM
    nqt = seq // tq
    units = batch * n_heads * nqt
    vw = 2 * V_DIM

    def unit(u):
        bh = u // nqt
        return bh // n_heads, bh % n_heads, u % nqt

    def cur(t):
        return unit(jnp.minimum(t, units - 1))

    def prv(t):
        return unit(jnp.clip(t - 1, 0, units - 1))

    def prv2(t):
        return unit(jnp.maximum(t - 2, 0))

    def q_map(t):
        b, h, i = cur(t)
        return (b * nqt + i, h)

    def k_map(t):
        b, h, _ = cur(t)
        return (b, n_heads + h)

    def km_map(t):
        _, h, _ = cur(t)
        return (0, n_heads + h)

    def v_map(t):
        b, h, _ = prv(t)
        return (b, h)

    def vm_map(t):
        _, h, _ = prv(t)
        return (0, h)

    def o_map(t):
        b, h, i = prv2(t)
        return (b * nqt + i, h)

    steps = units + 2
    n_cast = len(cast_ws)
    cast_2d, cast_specs, cast_shapes = [], [], []
    if n_cast:
        cast_blocks = 1 << (steps.bit_length() - 1)
        for w in cast_ws:
            w2 = w.reshape(-1, w.shape[-1])
            rows = w2.shape[0] // cast_blocks
            assert rows * cast_blocks == w2.shape[0] and rows % 16 == 0, (w.shape, cast_blocks)
            spec = pl.BlockSpec((rows, w2.shape[1]),
                                lambda t: (jnp.minimum(t, cast_blocks - 1), 0))
            cast_2d.append(w2)
            cast_specs.append(spec)
            cast_shapes.append(jax.ShapeDtypeStruct(w2.shape, BF16))

    outs = pl.pallas_call(
        functools.partial(_attn_kernel, n_cast=n_cast),
        grid=(steps,),
        in_specs=[
            pl.BlockSpec((1, 1), lambda t: (0, 0)),
            pl.BlockSpec((tq, V_DIM), q_map),
            pl.BlockSpec((seq, V_DIM), k_map),
            pl.BlockSpec((META_ROWS, V_DIM), km_map),
            pl.BlockSpec((seq, vw), v_map),
            pl.BlockSpec((META_ROWS, vw), vm_map),
            pl.BlockSpec((1, V_DIM), lambda t: (0, 0)),
        ] + cast_specs,
        out_specs=[pl.BlockSpec((tq, V_DIM), o_map)] + cast_specs,
        out_shape=[jax.ShapeDtypeStruct((n, two_w // 2), BF16)] + cast_shapes,
        scratch_shapes=(
            [pltpu.VMEM((2 * tq, seq), F32)] * 2
            + [pltpu.VMEM((2 * tq, META_ROWS), F32)] * 2
            + [pltpu.VMEM((2 * tq, LANES), F32)] * 2
            + [pltpu.VMEM((2 * tq, vw), F32)] * 2
        ),
        compiler_params=_cparams(("arbitrary",)),
        name="attn",
    )(lam, qk, qk, qk_meta, vx, vx_meta, subln_w.reshape(1, V_DIM).astype(F32), *cast_2d)
    return outs[0], [o.reshape(w.shape) for o, w in zip(outs[1:], cast_ws)]


def _pool_kernel(p_ref, prev_ref, next_ref, pm_ref, w_ref, sc_ref, o_ref, *, tps, total_t):
    i = pl.program_id(0)
    tm = p_ref.shape[0]
    ti = i % tps
    p = p_ref[...]
    prev = jnp.where(ti == 0, pm_ref[...], prev_ref[...])
    nxt = jnp.where(ti == tps - 1, 0.0, next_ref[...])
    ext = jnp.concatenate([prev, p, nxt], axis=0)
    rows = tm + 2 * HALO
    shift = lambda t, k: pltpu.roll(t, rows - k, 0)
    t_pos = N_META + ti * tm + lax.broadcasted_iota(jnp.int32, (tm, 1), 0)
    pch = p.shape[1] // len(POOL_WINDOWS)
    for g, w in enumerate(POOL_WINDOWS):
        sl = slice(g * pch, (g + 1) * pch)
        acc = ext[:, sl]
        span = 1
        while span < w:
            acc = acc + shift(acc, span)
            span *= 2
        off = HALO - w // 2
        win = (shift(acc, off) if off else acc)[:tm]
        cnt = jnp.minimum(w, total_t - t_pos + w // 2).astype(F32)
        z = (win / cnt - p[:, sl]).astype(BF16)
        y = jnp.dot(z, w_ref[g], preferred_element_type=F32)
        o_ref[:, sl] = (y * sc_ref[:, sl]).astype(BF16)


def _pool(p, p_meta, pool_w_bf, pool_scale, seq, tm):
    n, pw = p.shape
    tps = seq // tm
    hb = tm // HALO
    nb = n // HALO
    kern = functools.partial(_pool_kernel, tps=tps, total_t=seq + N_META)
    return pl.pallas_call(
        kern,
        grid=(n // tm,),
        in_specs=[
            pl.BlockSpec((tm, pw), lambda i: (i, 0)),
            pl.BlockSpec((HALO, pw), lambda i: (jnp.maximum(i * hb - 1, 0), 0)),
            pl.BlockSpec((HALO, pw), lambda i: (jnp.minimum((i + 1) * hb, nb - 1), 0)),
            pl.BlockSpec((HALO, pw), lambda i: (N_META // HALO - 1, 0)),
            pl.BlockSpec(pool_w_bf.shape, lambda i: (0, 0, 0)),
            pl.BlockSpec((1, pw), lambda i: (0, 0)),
        ],
        out_specs=pl.BlockSpec((tm, pw), lambda i: (i, 0)),
        out_shape=jax.ShapeDtypeStruct((n, pw), BF16),
        compiler_params=_cparams(("arbitrary",)),
        name="pool",
    )(p, p, p, p_meta, pool_w_bf, pool_scale.reshape(1, pw).astype(F32))


def _outproj_kernel(a_ref, p_ref, wa_ref, wp_ref, x_ref, o_ref):
    acc = jnp.dot(a_ref[...], wa_ref[...], preferred_element_type=F32)
    acc = acc + jnp.dot(p_ref[...], wp_ref[...], preferred_element_type=F32)
    o_ref[...] = x_ref[...] + acc


def _outproj(attn, pool, w_out_bf, x2d, tm, tn):
    n, d = x2d.shape
    half = attn.shape[1]
    return pl.pallas_call(
        _outproj_kernel,
        grid=(n // tm, d // tn),
        in_specs=[
            pl.BlockSpec((tm, half), lambda i, j: (i, 0)),
            pl.BlockSpec((tm, half), lambda i, j: (i, 0)),
            pl.BlockSpec((half, tn), lambda i, j: (0, j)),
            pl.BlockSpec((half, tn), lambda i, j: (1, j)),
            pl.BlockSpec((tm, tn), lambda i, j: (i, j)),
        ],
        out_specs=pl.BlockSpec((tm, tn), lambda i, j: (i, j)),
        out_shape=jax.ShapeDtypeStruct((n, d), F32),
        compiler_params=_cparams(("arbitrary", "arbitrary")),
        name="outproj",
    )(attn, pool, w_out_bf, w_out_bf, x2d)


def _route_rows(lg):
    lane = lax.broadcasted_iota(jnp.int32, lg.shape, 1).astype(F32)
    ninf = -jnp.inf
    nolane = float(LANES)
    rmax = lambda t: jnp.max(t, axis=-1, keepdims=True)
    rmin = lambda t: jnp.min(t, axis=-1, keepdims=True)
    is_g = lane < N_GROUPS
    gmax = rmax(jnp.where(is_g, lg, ninf))
    gsel = rmin(jnp.where(is_g & (lg == gmax), lane, nolane))
    p_g = 1.0 / jnp.sum(jnp.where(is_g, jnp.exp(lg - gmax), 0.0), axis=-1, keepdims=True)
    lo = N_GROUPS + EXPERTS_PER_GROUP * gsel
    in_e = (lane >= lo) & (lane < lo + EXPERTS_PER_GROUP)
    v1 = rmax(jnp.where(in_e, lg, ninf))
    i1 = rmin(jnp.where(in_e & (lg == v1), lane, nolane))
    in_e2 = in_e & (lane != i1)
    v2 = rmax(jnp.where(in_e2, lg, ninf))
    i2 = rmin(jnp.where(in_e2 & (lg == v2), lane, nolane))
    t = jnp.exp(v2 - v1)
    w1 = p_g / (1.0 + t)
    w2 = w1 * t
    out = jnp.where(lane == 0, i1 - N_GROUPS, 0.0)
    out = jnp.where(lane == 1, i2 - N_GROUPS, out)
    out = jnp.where(lane == 2, w1, out)
    return jnp.where(lane == 3, w2, out)


def _store_token_major(ref, x):
    rows, k = x.shape[0], x.shape[1] // LANES
    for c in range(k):
        ref[pl.ds(c, rows, stride=k), :] = x[:, c * LANES:(c + 1) * LANES]


def _load_token_major(ref, rows):
    k = ref.shape[0] // rows
    return [ref[pl.ds(c, rows, stride=k), :] for c in range(k)]


def _store_packed_rows(ref, x_bf):
    half = x_bf.shape[1] // 2
    lo = pltpu.bitcast(x_bf[:, :half].astype(F32), jnp.uint32)
    hi = pltpu.bitcast(x_bf[:, half:].astype(F32), jnp.uint32)
    _store_token_major(ref, (lo >> 16) | (hi & jnp.uint32(0xFFFF0000)))


def _load_packed_rows(ref, rows):
    words = _load_token_major(ref, rows)
    lo = [pltpu.bitcast(w << 16, F32).astype(BF16) for w in words]
    hi = [pltpu.bitcast(w & jnp.uint32(0xFFFF0000), F32).astype(BF16) for w in words]
    return jnp.concatenate(lo + hi, axis=1)


def _router_kernel(ha_ref, hb_ref, nw_ref, wr_ref, hn_ref, rt_ref, *, na_tiles):
    i = pl.program_id(0)

    def body(h_ref):
        h = h_ref[...]
        ms = jnp.mean(h * h, axis=-1, keepdims=True)
        hn = (h * lax.rsqrt(ms + NORM_EPS) * nw_ref[...]).astype(BF16)
        _store_packed_rows(hn_ref, hn)
        lg = jnp.dot(hn, wr_ref[...], preferred_element_type=F32)
        rt_ref[...] = _route_rows(lg)

    @pl.when(i < na_tiles)
    def _():
        body(ha_ref)

    @pl.when(i >= na_tiles)
    def _():
        body(hb_ref)


def _router(h1a, h1b, norm_w, wr_bf, tm):
    (n_a, d), n_b = h1a.shape, h1b.shape[0]
    ta, tb = n_a // tm, n_b // tm
    n = n_a + n_b
    kw = d // 2 // LANES
    return pl.pallas_call(
        functools.partial(_router_kernel, na_tiles=ta),
        grid=(ta + tb,),
        in_specs=[
            pl.BlockSpec((tm, d), lambda i: (jnp.minimum(i, ta - 1), 0)),
            pl.BlockSpec((tm, d), lambda i: (jnp.maximum(i - ta, 0), 0)),
            pl.BlockSpec((1, d), lambda i: (0, 0)),
            pl.BlockSpec((d, LANES), lambda i: (0, 0)),
        ],
        out_specs=[
            pl.BlockSpec((tm * kw, LANES), lambda i: (i, 0)),
            pl.BlockSpec((tm, LANES), lambda i: (i, 0)),
        ],
        out_shape=[
            jax.ShapeDtypeStruct((n * kw, LANES), jnp.uint32),
            jax.ShapeDtypeStruct((n, LANES), F32),
        ],
        compiler_params=_cparams(("arbitrary",)),
        name="router",
    )(h1a, h1b, norm_w.reshape(1, d), wr_bf)


def _route(rt, bm):
    n = rt.shape[0]
    eid = rt[:, :TOP_K].astype(jnp.int32).reshape(-1)
    wts = rt[:, TOP_K:2 * TOP_K].reshape(-1)
    na = n * TOP_K
    onehot = (eid[:, None] == jnp.arange(N_EXPERTS, dtype=jnp.int32)[None, :]).astype(jnp.int32)
    csum = jnp.cumsum(onehot, axis=0)
    rank = jnp.take_along_axis(csum, eid[:, None], axis=1)[:, 0] - 1
    counts = csum[-1]
    pcounts = ((counts + bm - 1) // bm) * bm
    pends = jnp.cumsum(pcounts)
    pstarts = pends - pcounts
    dest = pstarts[eid] + rank
    n_blocks = -(-na // bm) + N_EXPERTS
    rows = n_blocks * bm
    aidx = jnp.arange(na, dtype=jnp.int32)
    inv = jnp.full((rows,), -1, jnp.int32).at[dest].set(aidx)
    valid = inv >= 0
    src = jnp.maximum(inv, 0)
    tok = src // TOP_K
    slot = jnp.where(valid, (src % TOP_K) * n + src // TOP_K,
                     na + jnp.arange(rows, dtype=jnp.int32) % bm)
    wrow = jnp.where(valid, wts[src], 0.0)
    bstart = jnp.arange(n_blocks, dtype=jnp.int32) * bm
    block_e = jnp.minimum(jnp.sum(bstart[:, None] >= pends[None, :], axis=1),
                          N_EXPERTS - 1).astype(jnp.int32)
    nvalid = jnp.clip(pstarts[block_e] + counts[block_e] - bstart, 0, bm).astype(jnp.int32)
    prev_e = jnp.concatenate([jnp.full((1,), -1, jnp.int32), block_e[:-1]])
    first = ((nvalid > 0) & (block_e != prev_e)).astype(jnp.int32)
    wslot = (jnp.cumsum(first) - 1) % 2
    ids = jnp.arange(N_EXPERTS, dtype=jnp.int32)
    later = (ids[None, :] > ids[:, None]) & (counts[None, :] > 0)
    next_nonempty = jnp.min(jnp.where(later, ids[None, :], N_EXPERTS), axis=1)
    next_e = jnp.where(next_nonempty < N_EXPERTS, next_nonempty, -1)[block_e]
    meta = jnp.stack([block_e, nvalid, first, wslot.astype(jnp.int32), next_e.astype(jnp.int32)])
    return meta, tok, slot, wrow.reshape(rows, 1), n_blocks


def _start_row_gather(tok_ref, x_hbm, buf, sem, blk, bm):
    k = buf.shape[0] // bm
    for r in range(bm):
        t = pl.multiple_of(tok_ref[blk * bm + r] * k, k)
        pltpu.make_async_copy(x_hbm.at[pl.ds(t, k)], buf.at[pl.ds(r * k, k)], sem).start(
            priority=r % 2)


def _wait_row_gather(x_hbm, buf, sem, bm):
    k = buf.shape[0] // bm
    for r in range(bm):
        pltpu.make_async_copy(x_hbm.at[pl.ds(0, k)], buf.at[pl.ds(r * k, k)], sem).wait()


META_E, META_NV, META_FIRST, META_WSLOT, META_NEXT = range(5)


def _expert_weights(meta_ref, i, pairs, sem):
    first = meta_ref[META_FIRST, i] == 1
    slot = meta_ref[META_WSLOT, i]
    nxt = meta_ref[META_NEXT, i]

    def copies(e, s):
        return [pltpu.make_async_copy(w.at[e], buf.at[s], sem.at[s]) for w, buf in pairs]

    @pl.when(i == 0)
    def _():
        for c in copies(meta_ref[META_E, 0], 0):
            c.start(priority=1)

    @pl.when(first)
    def _():
        for c in copies(0, slot):
            c.wait()

    @pl.when(first & (nxt >= 0))
    def _():
        for c in copies(nxt, 1 - slot):
            c.start(priority=1)

    return slot


RING = 3


def _gateup_kernel(meta_ref, tok_ref, x_hbm, wg_hbm, wu_hbm, h_ref,
                   xbuf, wg_buf, wu_buf, sem, wsem, *, nb):
    i = pl.program_id(0)
    bm = h_ref.shape[0]
    nv = meta_ref[META_NV, i]
    nv_prev = meta_ref[META_NV, jnp.maximum(i - 1, 0)]
    ws = _expert_weights(meta_ref, i, [(wg_hbm, wg_buf), (wu_hbm, wu_buf)], wsem)

    def start(blk):
        s = blk % RING
        _start_row_gather(tok_ref, x_hbm, xbuf.at[s], sem.at[s], jnp.minimum(blk, nb - 1), bm)

    def wait(blk):
        s = blk % RING
        _wait_row_gather(x_hbm, xbuf.at[s], sem.at[s], bm)

    @pl.when(i == 0)
    def _():
        start(0)
        start(1)

    @pl.when(nv > 0)
    def _():
        wait(i)
        start(i + 2)
        x = _load_packed_rows(xbuf.at[i % RING], bm)
        g = jnp.dot(x, wg_buf[ws], preferred_element_type=F32)
        u = jnp.dot(x, wu_buf[ws], preferred_element_type=F32)
        h_ref[...] = (g * (1.0 / (1.0 + jnp.exp(-g))) * u).astype(BF16)

    @pl.when((nv == 0) & (nv_prev > 0))
    def _():
        wait(i)
        wait(i + 1)

    @pl.when((i == nb - 1) & (nv > 0))
    def _():
        wait(nb)
        wait(nb + 1)

    @pl.when(nv == 0)
    def _():
        h_ref[...] = jnp.zeros_like(h_ref)


def _gateup(meta, tok, hn2p, wg_bf, wu_bf, n_blocks, bm):
    d, f = wg_bf.shape[1], wg_bf.shape[2]
    kw = d // 2 // LANES
    anyspec = pl.BlockSpec(memory_space=pl.ANY)
    grid_spec = pltpu.PrefetchScalarGridSpec(
        num_scalar_prefetch=2,
        grid=(n_blocks,),
        in_specs=[anyspec, anyspec, anyspec],
        out_specs=pl.BlockSpec((bm, f), lambda i, mt, tk: (i, 0)),
        scratch_shapes=[pltpu.VMEM((RING, bm * kw, LANES), jnp.uint32),
                        pltpu.VMEM((2, d, f), BF16), pltpu.VMEM((2, d, f), BF16),
                        pltpu.SemaphoreType.DMA((RING,)), pltpu.SemaphoreType.DMA((2,))],
    )
    return pl.pallas_call(
        functools.partial(_gateup_kernel, nb=n_blocks),
        grid_spec=grid_spec,
        out_shape=jax.ShapeDtypeStruct((n_blocks * bm, f), BF16),
        compiler_params=_cparams(("arbitrary",)),
        name="gateup",
    )(meta, tok, hn2p, wg_bf, wu_bf)


def _start_row_scatter(slot_ref, buf, y_hbm, sem, blk, bm):
    k = buf.shape[0] // bm
    for r in range(bm):
        s = pl.multiple_of(slot_ref[blk * bm + r] * k, k)
        pltpu.make_async_copy(buf.at[pl.ds(r * k, k)], y_hbm.at[pl.ds(s, k)], sem).start(
            priority=r % 2)


def _wait_row_scatter(buf, y_hbm, sem, bm):
    k = buf.shape[0] // bm
    for r in range(bm):
        pltpu.make_async_copy(buf.at[pl.ds(r * k, k)], y_hbm.at[pl.ds(0, k)], sem).wait()


def _down_kernel(meta_ref, slot_ref, h_ref, wd_hbm, w_ref, y_hbm, ybuf, wd_buf, sem, wsem,
                 *, nb):
    i = pl.program_id(0)
    bm = h_ref.shape[0]
    nv = meta_ref[META_NV, i]
    nv_prev = meta_ref[META_NV, jnp.maximum(i - 1, 0)]
    first_empty = (nv == 0) & (nv_prev > 0)
    ws = _expert_weights(meta_ref, i, [(wd_hbm, wd_buf)], wsem)

    def compute():
        y = jnp.dot(h_ref[...], wd_buf[ws], preferred_element_type=F32)
        ybuf[i % RING] = y * w_ref[...]

    def start(blk):
        s = blk % RING
        _start_row_scatter(slot_ref, ybuf.at[s], y_hbm, sem.at[s], blk, bm)

    def wait(blk):
        s = blk % RING
        _wait_row_scatter(ybuf.at[s], y_hbm, sem.at[s], bm)

    @pl.when(i == 0)
    def _():
        rows = ybuf.shape[1]
        ybuf[1] = jnp.zeros(ybuf.shape[1:], F32)
        spare = pltpu.make_async_copy(ybuf.at[1], y_hbm.at[pl.ds(y_hbm.shape[0] - rows, rows)],
                                      sem.at[1])
        spare.start()
        spare.wait()
        compute()

    @pl.when((i >= 3) & ((nv > 0) | first_empty))
    def _():
        wait(i - 3)

    @pl.when((i >= 1) & (nv > 0))
    def _():
        start(i - 1)
        compute()

    @pl.when((i >= 2) & first_empty)
    def _():
        wait(i - 2)

    @pl.when(first_empty)
    def _():
        start(i - 1)
        wait(i - 1)

    @pl.when((i == nb - 1) & (nv > 0))
    def _():
        if nb > 2:
            wait(nb - 3)
        if nb > 1:
            wait(nb - 2)
        start(nb - 1)
        wait(nb - 1)


def _down(meta, slot, h, wd_bf, wrow, n_assign, n_blocks, bm):
    f, d = wd_bf.shape[1], wd_bf.shape[2]
    grid_spec = pltpu.PrefetchScalarGridSpec(
        num_scalar_prefetch=2,
        grid=(n_blocks,),
        in_specs=[
            pl.BlockSpec((bm, f), lambda i, mt, sl: (i, 0)),
            pl.BlockSpec(memory_space=pl.ANY),
            pl.BlockSpec((bm, 1), lambda i, mt, sl: (i, 0)),
        ],
        out_specs=pl.BlockSpec(memory_space=pl.ANY),
        scratch_shapes=[pltpu.VMEM((RING, bm, d), F32),
                        pltpu.VMEM((2, f, d), BF16),
                        pltpu.SemaphoreType.DMA((RING,)), pltpu.SemaphoreType.DMA((2,))],
    )
    return pl.pallas_call(
        functools.partial(_down_kernel, nb=n_blocks),
        grid_spec=grid_spec,
        out_shape=jax.ShapeDtypeStruct((n_assign + bm, d), F32),
        compiler_params=_cparams(("arbitrary",)),
        name="down",
    )(meta, slot, h, wd_bf, wrow)


def _final_kernel(h_ref, y0_ref, y1_ref, nw_ref, o_ref):
    h = h_ref[...] + (y0_ref[...] + y1_ref[...])
    ms = jnp.mean(h * h, axis=-1, keepdims=True)
    o_ref[...] = h * lax.rsqrt(ms + NORM_EPS) * nw_ref[...]


def _final(h1, yc, norm_w, tm, row_off, n_tot):
    n, d = h1.shape
    nt = n // tm
    o0 = row_off // tm
    o1 = (n_tot + row_off) // tm
    return pl.pallas_call(
        _final_kernel,
        grid=(nt,),
        in_specs=[
            pl.BlockSpec((tm, d), lambda i: (i, 0)),
            pl.BlockSpec((tm, d), lambda i: (i + o0, 0)),
            pl.BlockSpec((tm, d), lambda i: (i + o1, 0)),
            pl.BlockSpec((1, d), lambda i: (0, 0)),
        ],
        out_specs=pl.BlockSpec((tm, d), lambda i: (i, 0)),
        out_shape=jax.ShapeDtypeStruct((n, d), F32),
        compiler_params=_cparams(("arbitrary",)),
        name="final",
    )(h1, yc, yc, norm_w.reshape(1, d))


def _tiles(seq, d):
    tm = min(512, seq)
    return dict(tm=tm, tn=min(512, d // 2), tq=min(256, seq), tm_out=tm,
                tn_out=min(1024, d), tm_fin=min(256, seq), bm=min(256, seq))


def _mixer(x, shared, prm, cast_ws):
    b, seq, d = x.shape
    n = b * seq
    t = _tiles(seq, d)
    x2d = x.reshape(n, d)
    qk, v, p = _inproj(x2d, prm["attn_norm_w"], shared["w_in"], shared["rope_real"], seq,
                       t["tm"], t["tn"])
    attn, cast = _attention(shared["lam"], qk, v, shared["qk_meta"], shared["v_meta"],
                            prm["subln_w"], b, seq, t["tq"], cast_ws)
    pool = _pool(p, shared["p_meta"], shared["pool_w"], prm["pool_scale"], seq, t["tm"])
    return _outproj(attn, pool, shared["w_out"], x2d, t["tm_out"], t["tn_out"]), cast


def _moe_and_final(h1a, h1b, shapes, shared, prm):
    seq, d = shapes[0][1], shapes[0][2]
    t = _tiles(seq, d)
    bm = t["bm"]
    n_a, n_b = h1a.shape[0], h1b.shape[0]
    n = n_a + n_b
    hn2, rt = _router(h1a, h1b, prm["ffn_norm_w"], shared["w_router"], t["tm"])
    meta, tok, slot, wrow, n_blocks = _route(rt, bm)
    h = _gateup(meta, tok, hn2, shared["w_gate"], shared["w_up"], n_blocks, bm)
    yc = _down(meta, slot, h, shared["w_down"], wrow, n * TOP_K, n_blocks, bm)
    out_a = _final(h1a, yc, prm["final_norm_w"], t["tm_fin"], 0, n)
    out_b = _final(h1b, yc, prm["final_norm_w"], t["tm_fin"], n_a, n)
    return out_a.reshape(shapes[0]), out_b.reshape(shapes[1])


def kernel(x_prompt, x_sample, meta_tokens, attn_norm_w, w_in, lambda_q1, lambda_k1, lambda_q2, lambda_k2, subln_w, pool_w, pool_scale, w_out, ffn_norm_w, router_group, router_expert, expert_w_gate, expert_w_up, expert_w_down, final_norm_w):
    d = x_prompt.shape[-1]
    seq = x_prompt.shape[1]
    assert x_sample.shape[1] == seq and attn_norm_w.shape[0] == 1
    prm = dict(attn_norm_w=attn_norm_w[0], subln_w=subln_w[0], pool_scale=pool_scale[0],
               ffn_norm_w=ffn_norm_w[0], final_norm_w=final_norm_w)
    wr = jnp.concatenate([router_group[0], router_expert[0]], axis=1)
    wr = jnp.pad(wr, ((0, 0), (0, LANES - wr.shape[1])))
    t = _tiles(seq, d)
    shared = dict(
        w_in=_prep_w_in(w_in[0], t["tn"]), w_out=w_out[0].astype(BF16), pool_w=pool_w[0].astype(BF16),
        w_router=wr.astype(BF16),
        lam=_lam(lambda_q1[0], lambda_k1[0], lambda_q2[0], lambda_k2[0]),
        rope_real=_rope_tables(jnp.arange(N_META, N_META + seq, dtype=F32)),
    )
    meta_pad = jnp.pad(meta_tokens.astype(F32), ((0, META_ROWS - N_META), (0, 0)))
    qk_m, v_m, p_m = _inproj(meta_pad, prm["attn_norm_w"], shared["w_in"],
                             _rope_tables(jnp.arange(META_ROWS, dtype=F32)), META_ROWS,
                             META_ROWS, t["tn"])
    shared.update(qk_meta=qk_m, v_meta=v_m, p_meta=p_m)
    h1a, (w_down_bf,) = _mixer(x_prompt, shared, prm, (expert_w_down[0],))
    h1b, (w_gate_bf, w_up_bf) = _mixer(x_sample, shared, prm,
                                       (expert_w_gate[0], expert_w_up[0]))
    shared.update(w_gate=w_gate_bf, w_up=w_up_bf, w_down=w_down_bf)
    return _moe_and_final(h1a, h1b, (x_prompt.shape, x_sample.shape), shared, prm)
```

```python
import functools
import math

import jax
import jax.numpy as jnp
from jax import lax
from jax.experimental import pallas as pl
from jax.experimental.pallas import tpu as pltpu

F32 = jnp.float32
BF16 = jnp.bfloat16

N_META = 16
HEAD_DIM = 64
V_DIM = 2 * HEAD_DIM
ROT_DIM = HEAD_DIM // 4
ROPE_THETA = 500000.0
ATTN_SCALE = HEAD_DIM ** -0.5
SUBLN_EPS = 1e-5
NORM_EPS = 1e-6
POOL_WINDOWS = (2, 4, 8, 16)
N_GROUPS = 4
EXPERTS_PER_GROUP = 8
N_EXPERTS = N_GROUPS * EXPERTS_PER_GROUP
TOP_K = 2
LAMBDA_INIT = 0.8 - 0.6 * math.exp(-0.3 * 0)

LANES = 128
SUBLANES = 8
META_ROWS = 128
HALO = 8
MASK_VALUE = -1e30

VMEM_LIMIT = 56 * 1024 * 1024


def _cparams(sem):
    return pltpu.CompilerParams(dimension_semantics=sem, vmem_limit_bytes=VMEM_LIMIT)


def _inproj_kernel(x_ref, nw_ref, w_ref, c_ref, s_ref,
                   qk_ref, v_ref, p_ref, hn_ref, acc_a, acc_b, *, nqk, nv, ncol, nsteps):
    t = pl.program_id(0)
    j = t % ncol
    jp = jnp.maximum(t - 1, 0) % ncol

    @pl.when(t == 0)
    def _():
        acc_b[...] = jnp.zeros_like(acc_b)

    @pl.when((j == 0) & (t < nsteps))
    def _():
        x = x_ref[...]
        ms = jnp.mean(x * x, axis=-1, keepdims=True)
        hn_ref[...] = (x * lax.rsqrt(ms + NORM_EPS) * nw_ref[...]).astype(BF16)

    def matmul(acc_w):
        acc_w[...] = jnp.dot(hn_ref[...], w_ref[...], preferred_element_type=F32)

    def epilogue_qk(acc_r):
        scale = jnp.where(jp < nqk // 2, ATTN_SCALE, 1.0).astype(F32)
        c = c_ref[...] * scale
        s = s_ref[...] * scale
        for u in range(acc_r.shape[1] // LANES):
            blk = acc_r[:, u * LANES:(u + 1) * LANES]
            r = blk * c + pltpu.roll(blk, LANES // 2, 1) * s
            qk_ref[:, u * LANES:(u + 1) * LANES] = r.astype(BF16)

    def epilogue_v(acc_r):
        lane = lax.broadcasted_iota(jnp.int32, (acc_r.shape[0], LANES), 1)
        ones_col = jnp.where(lane == 0, 1.0, 0.0).astype(BF16)
        for u in range(acc_r.shape[1] // LANES):
            v_ref[:, 2 * u * LANES:(2 * u + 1) * LANES] = acc_r[:, u * LANES:(u + 1) * LANES].astype(BF16)
            v_ref[:, (2 * u + 1) * LANES:(2 * u + 2) * LANES] = ones_col

    def epilogue_p(acc_r):
        p_ref[...] = acc_r[...]

    sections = ((jp < nqk, epilogue_qk),
                ((jp >= nqk) & (jp < nqk + nv), epilogue_v),
                (jp >= nqk + nv, epilogue_p))
    for parity, (acc_w, acc_r) in enumerate(((acc_a, acc_b), (acc_b, acc_a))):
        for in_section, epilogue in sections:
            @pl.when((t % 2 == parity) & in_section)
            def _(acc_w=acc_w, acc_r=acc_r, epilogue=epilogue):
                epilogue(acc_r)
                matmul(acc_w)


_HALF = ROT_DIM // 2
HEAD_LANE_SRC = (list(range(0, _HALF)) + list(range(HEAD_DIM, HEAD_DIM + _HALF))
                 + list(range(ROT_DIM, HEAD_DIM))
                 + list(range(_HALF, ROT_DIM)) + list(range(HEAD_DIM + _HALF, HEAD_DIM + ROT_DIM))
                 + list(range(HEAD_DIM + ROT_DIM, 2 * HEAD_DIM)))
COMPONENT0_LANES = tuple(i for i, src in enumerate(HEAD_LANE_SRC) if src < HEAD_DIM)


def _prep_w_in_kernel(w_ref, perm_ref, o_ref, *, nqk):
    j = pl.program_id(1)

    @pl.when(j < nqk)
    def _():
        for t in range(w_ref.shape[1] // LANES):
            sl = slice(t * LANES, (t + 1) * LANES)
            o_ref[:, sl] = jnp.dot(w_ref[:, sl].astype(BF16), perm_ref[...],
                                   preferred_element_type=F32).astype(BF16)

    @pl.when(j >= nqk)
    def _():
        o_ref[...] = w_ref[...].astype(BF16)


def _prep_w_in(w_in, tn):
    d, d_in = w_in.shape
    nqk = d // tn
    perm = jnp.zeros((V_DIM, V_DIM), F32).at[jnp.array(HEAD_LANE_SRC), jnp.arange(V_DIM)].set(1.0)
    tr = min(1024, d)
    return pl.pallas_call(
        functools.partial(_prep_w_in_kernel, nqk=nqk),
        grid=(d // tr, d_in // tn),
        in_specs=[pl.BlockSpec((tr, tn), lambda i, j: (i, j)),
                  pl.BlockSpec((V_DIM, V_DIM), lambda i, j: (0, 0))],
        out_specs=pl.BlockSpec((tr, tn), lambda i, j: (i, j)),
        out_shape=jax.ShapeDtypeStruct((d, d_in), BF16),
        compiler_params=_cparams(("arbitrary", "arbitrary")),
        name="prep_w_in",
    )(w_in, perm.astype(BF16))


def _rope_tables(pos):
    inv_freq = ROPE_THETA ** (-jnp.arange(0, ROT_DIM, 2, dtype=F32) / ROT_DIM)
    ang = pos[:, None] * inv_freq[None, :]
    cos, sin = jnp.cos(ang), jnp.sin(ang)
    r = pos.shape[0]
    ones = jnp.ones((r, HEAD_DIM - ROT_DIM), F32)
    zeros = jnp.zeros((r, HEAD_DIM - ROT_DIM), F32)
    c = jnp.concatenate([cos, cos, ones, cos, cos, ones], axis=1)
    s = jnp.concatenate([-sin, -sin, zeros, sin, sin, zeros], axis=1)
    return c, s


def _inproj(x2d, norm_w, w_in_bf, tables, rows_per_seq, tm, tn):
    n, d = x2d.shape
    d_in = w_in_bf.shape[1]
    attn_w = d // 2
    nqk = 2 * attn_w // tn
    nv = attn_w // tn
    npool = (d_in - 3 * attn_w) // tn
    tps = rows_per_seq // tm
    c, s = tables
    nrow, ncol = n // tm, d_in // tn
    nsteps = nrow * ncol

    def prev(t):
        tp = jnp.maximum(t - 1, 0)
        return tp // ncol, tp % ncol

    tab_spec = pl.BlockSpec((tm, LANES), lambda t: (prev(t)[0] % tps, 0))
    kern = functools.partial(_inproj_kernel, nqk=nqk, nv=nv, ncol=ncol, nsteps=nsteps)
    return pl.pallas_call(
        kern,
        grid=(nsteps + 1,),
        in_specs=[
            pl.BlockSpec((tm, d), lambda t: (jnp.minimum(t // ncol, nrow - 1), 0)),
            pl.BlockSpec((1, d), lambda t: (0, 0)),
            pl.BlockSpec((d, tn), lambda t: (0, t % ncol)),
            tab_spec, tab_spec,
        ],
        out_specs=[
            pl.BlockSpec((tm, tn), lambda t: (prev(t)[0], jnp.minimum(prev(t)[1], nqk - 1))),
            pl.BlockSpec((tm, 2 * tn),
                         lambda t: (prev(t)[0], jnp.clip(prev(t)[1] - nqk, 0, nv - 1))),
            pl.BlockSpec((tm, tn),
                         lambda t: (prev(t)[0], jnp.clip(prev(t)[1] - nqk - nv, 0, npool - 1))),
        ],
        out_shape=[
            jax.ShapeDtypeStruct((n, 2 * attn_w), BF16),
            jax.ShapeDtypeStruct((n, 2 * attn_w), BF16),
            jax.ShapeDtypeStruct((n, d_in - 3 * attn_w), F32),
        ],
        scratch_shapes=[pltpu.VMEM((tm, d), BF16), pltpu.VMEM((tm, tn), F32),
                        pltpu.VMEM((tm, tn), F32)],
        compiler_params=_cparams(("arbitrary",)),
        name="inproj",
    )(x2d, norm_w.reshape(1, d), w_in_bf, c, s)


def _lam_kernel(q1_ref, k1_ref, q2_ref, k2_ref, o_ref):
    s1 = jnp.sum(q1_ref[...] * k1_ref[...], axis=-1, keepdims=True)
    s2 = jnp.sum(q2_ref[...] * k2_ref[...], axis=-1, keepdims=True)
    o_ref[...] = jnp.exp(s1) - jnp.exp(s2) + LAMBDA_INIT


def _lam(lq1, lk1, lq2, lk2):
    r = lambda t: t.reshape(1, -1).astype(F32)
    return pl.pallas_call(
        _lam_kernel,
        out_shape=jax.ShapeDtypeStruct((1, 1), F32),
        name="lam",
    )(r(lq1), r(lk1), r(lq2), r(lk2))


def _attn_step(refs, P, Q):
    lam_ref, q_ref, k_ref, km_ref, v_ref, vm_ref, sw_ref, o_ref, cast_in, cast_out = refs
    tq = q_ref.shape[0]

    for w_in, w_out in zip(cast_in, cast_out):
        w_out[...] = w_in[...].astype(BF16)

    oe = P["oe"][...]
    r = 1.0 / oe[:, V_DIM:V_DIM + 1]
    on = oe[:, :V_DIM] * r
    o = on[:tq] - lam_ref[...] * on[tq:]
    ms = jnp.mean(o * o, axis=-1, keepdims=True)
    y = o * lax.rsqrt(ms + SUBLN_EPS) * sw_ref[...]
    o_ref[...] = (y * (1.0 - LAMBDA_INIT)).astype(BF16)

    m_p = Q["m"][:, :1]
    e = jnp.exp(Q["s"][...] - m_p).astype(BF16)
    em = jnp.exp(Q["sm"][...] - m_p).astype(BF16)
    Q["oe"][...] = (jnp.dot(e, v_ref[...], preferred_element_type=F32)
                    + jnp.dot(em, vm_ref[...], preferred_element_type=F32))

    q = q_ref[...]
    lane = lax.broadcasted_iota(jnp.int32, q.shape, 1)
    zero = jnp.zeros_like(q)
    assert COMPONENT0_LANES == tuple(range(0, 8)) + tuple(range(16, 72))
    comp0 = (lane < 8) | ((lane >= 16) & (lane < 72))
    qs = jnp.concatenate([jnp.where(comp0, q, zero),
                          jnp.where(comp0, zero, q)], axis=0)
    dn = (((1,), (1,)), ((), ()))
    s = lax.dot_general(qs, k_ref[...], dn, preferred_element_type=F32)
    sm = lax.dot_general(qs, km_ref[...], dn, preferred_element_type=F32)
    col = lax.broadcasted_iota(jnp.int32, sm.shape, 1)
    sm = jnp.where(col < N_META, sm, MASK_VALUE)
    m = jnp.maximum(jnp.max(s, axis=-1, keepdims=True), jnp.max(sm, axis=-1, keepdims=True))
    P["s"][...] = s
    P["sm"][...] = sm
    P["m"][...] = jnp.broadcast_to(m, P["m"].shape)


ATTN_SCRATCH = ("s", "sm", "m", "oe")


def _attn_kernel(*refs, n_cast):
    lam_ref, q_ref, k_ref, km_ref, v_ref, vm_ref, sw_ref = refs[:7]
    cast_in = refs[7:7 + n_cast]
    o_ref = refs[7 + n_cast]
    cast_out = refs[8 + n_cast:8 + 2 * n_cast]
    scratch = refs[8 + 2 * n_cast:]
    A = dict(zip(ATTN_SCRATCH, scratch[0::2]))
    B = dict(zip(ATTN_SCRATCH, scratch[1::2]))
    t = pl.program_id(0)
    args = (lam_ref, q_ref, k_ref, km_ref, v_ref, vm_ref, sw_ref, o_ref, cast_in, cast_out)

    @pl.when(t == 0)
    def _():
        for name in ("s", "sm", "m"):
            B[name][...] = jnp.zeros_like(B[name])
        A["oe"][...] = jnp.ones_like(A["oe"])

    @pl.when(t % 2 == 0)
    def _():
        _attn_step(args, A, B)

    @pl.when(t % 2 == 1)
    def _():
        _attn_step(args, B, A)


def _attention(lam, qk, vx, qk_meta, vx_meta, subln_w, batch, seq, tq, cast_ws=()):
    n, two_w = qk.shape
    n_heads = two_w // 2 // V_DIM
    nqt = seq // tq
    units = batch * n_heads * nqt
    vw = 2 * V_DIM

    def unit(u):
        bh = u // nqt
        return bh // n_heads, bh % n_heads, u % nqt

    def cur(t):
        return unit(jnp.minimum(t, units - 1))

    def prv(t):
        return unit(jnp.clip(t - 1, 0, units - 1))

    def prv2(t):
        return unit(jnp.maximum(t - 2, 0))

    def q_map(t):
        b, h, i = cur(t)
        return (b * nqt + i, h)

    def k_map(t):
        b, h, _ = cur(t)
        return (b, n_heads + h)

    def km_map(t):
        _, h, _ = cur(t)
        return (0, n_heads + h)

    def v_map(t):
        b, h, _ = prv(t)
        return (b, h)

    def vm_map(t):
        _, h, _ = prv(t)
        return (0, h)

    def o_map(t):
        b, h, i = prv2(t)
        return (b * nqt + i, h)

    steps = units + 2
    n_cast = len(cast_ws)
    cast_2d, cast_specs, cast_shapes = [], [], []
    if n_cast:
        cast_blocks = 1 << (steps.bit_length() - 1)
        for w in cast_ws:
            w2 = w.reshape(-1, w.shape[-1])
            rows = w2.shape[0] // cast_blocks
            assert rows * cast_blocks == w2.shape[0] and rows % 16 == 0, (w.shape, cast_blocks)
            spec = pl.BlockSpec((rows, w2.shape[1]),
                                lambda t: (jnp.minimum(t, cast_blocks - 1), 0))
            cast_2d.append(w2)
            cast_specs.append(spec)
            cast_shapes.append(jax.ShapeDtypeStruct(w2.shape, BF16))

    outs = pl.pallas_call(
        functools.partial(_attn_kernel, n_cast=n_cast),
        grid=(steps,),
        in_specs=[
            pl.BlockSpec((1, 1), lambda t: (0, 0)),
            pl.BlockSpec((tq, V_DIM), q_map),
            pl.BlockSpec((seq, V_DIM), k_map),
            pl.BlockSpec((META_ROWS, V_DIM), km_map),
            pl.BlockSpec((seq, vw), v_map),
            pl.BlockSpec((META_ROWS, vw), vm_map),
            pl.BlockSpec((1, V_DIM), lambda t: (0, 0)),
        ] + cast_specs,
        out_specs=[pl.BlockSpec((tq, V_DIM), o_map)] + cast_specs,
        out_shape=[jax.ShapeDtypeStruct((n, two_w // 2), BF16)] + cast_shapes,
        scratch_shapes=(
            [pltpu.VMEM((2 * tq, seq), F32)] * 2
            + [pltpu.VMEM((2 * tq, META_ROWS), F32)] * 2
            + [pltpu.VMEM((2 * tq, LANES), F32)] * 2
            + [pltpu.VMEM((2 * tq, vw), F32)] * 2
        ),
        compiler_params=_cparams(("arbitrary",)),
        name="attn",
    )(lam, qk, qk, qk_meta, vx, vx_meta, subln_w.reshape(1, V_DIM).astype(F32), *cast_2d)
    return outs[0], [o.reshape(w.shape) for o, w in zip(outs[1:], cast_ws)]


def _pool_kernel(p_ref, prev_ref, next_ref, pm_ref, w_ref, sc_ref, o_ref, *, tps, total_t):
    i = pl.program_id(0)
    tm = p_ref.shape[0]
    ti = i % tps
    p = p_ref[...]
    prev = jnp.where(ti == 0, pm_ref[...], prev_ref[...])
    nxt = jnp.where(ti == tps - 1, 0.0, next_ref[...])
    ext = jnp.concatenate([prev, p, nxt], axis=0)
    rows = tm + 2 * HALO
    shift = lambda t, k: pltpu.roll(t, rows - k, 0)
    t_pos = N_META + ti * tm + lax.broadcasted_iota(jnp.int32, (tm, 1), 0)
    pch = p.shape[1] // len(POOL_WINDOWS)
    for g, w in enumerate(POOL_WINDOWS):
        sl = slice(g * pch, (g + 1) * pch)
        acc = ext[:, sl]
        span = 1
        while span < w:
            acc = acc + shift(acc, span)
            span *= 2
        off = HALO - w // 2
        win = (shift(acc, off) if off else acc)[:tm]
        cnt = jnp.minimum(w, total_t - t_pos + w // 2).astype(F32)
        z = (win / cnt - p[:, sl]).astype(BF16)
        y = jnp.dot(z, w_ref[g], preferred_element_type=F32)
        o_ref[:, sl] = (y * sc_ref[:, sl]).astype(BF16)


def _pool(p, p_meta, pool_w_bf, pool_scale, seq, tm):
    n, pw = p.shape
    tps = seq // tm
    hb = tm // HALO
    nb = n // HALO
    kern = functools.partial(_pool_kernel, tps=tps, total_t=seq + N_META)
    return pl.pallas_call(
        kern,
        grid=(n // tm,),
        in_specs=[
            pl.BlockSpec((tm, pw), lambda i: (i, 0)),
            pl.BlockSpec((HALO, pw), lambda i: (jnp.maximum(i * hb - 1, 0), 0)),
            pl.BlockSpec((HALO, pw), lambda i: (jnp.minimum((i + 1) * hb, nb - 1), 0)),
            pl.BlockSpec((HALO, pw), lambda i: (N_META // HALO - 1, 0)),
            pl.BlockSpec(pool_w_bf.shape, lambda i: (0, 0, 0)),
            pl.BlockSpec((1, pw), lambda i: (0, 0)),
        ],
        out_specs=pl.BlockSpec((tm, pw), lambda i: (i, 0)),
        out_shape=jax.ShapeDtypeStruct((n, pw), BF16),
        compiler_params=_cparams(("arbitrary",)),
        name="pool",
    )(p, p, p, p_meta, pool_w_bf, pool_scale.reshape(1, pw).astype(F32))


def _outproj_kernel(a_ref, p_ref, wa_ref, wp_ref, x_ref, o_ref):
    acc = jnp.dot(a_ref[...], wa_ref[...], preferred_element_type=F32)
    acc = acc + jnp.dot(p_ref[...], wp_ref[...], preferred_element_type=F32)
    o_ref[...] = x_ref[...] + acc


def _outproj(attn, pool, w_out_bf, x2d, tm, tn):
    n, d = x2d.shape
    half = attn.shape[1]
    return pl.pallas_call(
        _outproj_kernel,
        grid=(n // tm, d // tn),
        in_specs=[
            pl.BlockSpec((tm, half), lambda i, j: (i, 0)),
            pl.BlockSpec((tm, half), lambda i, j: (i, 0)),
            pl.BlockSpec((half, tn), lambda i, j: (0, j)),
            pl.BlockSpec((half, tn), lambda i, j: (1, j)),
            pl.BlockSpec((tm, tn), lambda i, j: (i, j)),
        ],
        out_specs=pl.BlockSpec((tm, tn), lambda i, j: (i, j)),
        out_shape=jax.ShapeDtypeStruct((n, d), F32),
        compiler_params=_cparams(("arbitrary", "arbitrary")),
        name="outproj",
    )(attn, pool, w_out_bf, w_out_bf, x2d)


def _route_rows(lg):
    lane = lax.broadcasted_iota(jnp.int32, lg.shape, 1).astype(F32)
    ninf = -jnp.inf
    nolane = float(LANES)
    rmax = lambda t: jnp.max(t, axis=-1, keepdims=True)
    rmin = lambda t: jnp.min(t, axis=-1, keepdims=True)
    is_g = lane < N_GROUPS
    gmax = rmax(jnp.where(is_g, lg, ninf))
    gsel = rmin(jnp.where(is_g & (lg == gmax), lane, nolane))
    p_g = 1.0 / jnp.sum(jnp.where(is_g, jnp.exp(lg - gmax), 0.0), axis=-1, keepdims=True)
    lo = N_GROUPS + EXPERTS_PER_GROUP * gsel
    in_e = (lane >= lo) & (lane < lo + EXPERTS_PER_GROUP)
    v1 = rmax(jnp.where(in_e, lg, ninf))
    i1 = rmin(jnp.where(in_e & (lg == v1), lane, nolane))
    in_e2 = in_e & (lane != i1)
    v2 = rmax(jnp.where(in_e2, lg, ninf))
    i2 = rmin(jnp.where(in_e2 & (lg == v2), lane, nolane))
    t = jnp.exp(v2 - v1)
    w1 = p_g / (1.0 + t)
    w2 = w1 * t
    out = jnp.where(lane == 0, i1 - N_GROUPS, 0.0)
    out = jnp.where(lane == 1, i2 - N_GROUPS, out)
    out = jnp.where(lane == 2, w1, out)
    return jnp.where(lane == 3, w2, out)


def _store_token_major(ref, x):
    rows, k = x.shape[0], x.shape[1] // LANES
    for c in range(k):
        ref[pl.ds(c, rows, stride=k), :] = x[:, c * LANES:(c + 1) * LANES]


def _load_token_major(ref, rows):
    k = ref.shape[0] // rows
    return [ref[pl.ds(c, rows, stride=k), :] for c in range(k)]


def _store_packed_rows(ref, x_bf):
    half = x_bf.shape[1] // 2
    lo = pltpu.bitcast(x_bf[:, :half].astype(F32), jnp.uint32)
    hi = pltpu.bitcast(x_bf[:, half:].astype(F32), jnp.uint32)
    _store_token_major(ref, (lo >> 16) | (hi & jnp.uint32(0xFFFF0000)))


def _load_packed_rows(ref, rows):
    words = _load_token_major(ref, rows)
    lo = [pltpu.bitcast(w << 16, F32).astype(BF16) for w in words]
    hi = [pltpu.bitcast(w & jnp.uint32(0xFFFF0000), F32).astype(BF16) for w in words]
    return jnp.concatenate(lo + hi, axis=1)


def _router_kernel(ha_ref, hb_ref, nw_ref, wr_ref, hn_ref, rt_ref, *, na_tiles):
    i = pl.program_id(0)

    def body(h_ref):
        h = h_ref[...]
        ms = jnp.mean(h * h, axis=-1, keepdims=True)
        hn = (h * lax.rsqrt(ms + NORM_EPS) * nw_ref[...]).astype(BF16)
        _store_packed_rows(hn_ref, hn)
        lg = jnp.dot(hn, wr_ref[...], preferred_element_type=F32)
        rt_ref[...] = _route_rows(lg)

    @pl.when(i < na_tiles)
    def _():
        body(ha_ref)

    @pl.when(i >= na_tiles)
    def _():
        body(hb_ref)


def _router(h1a, h1b, norm_w, wr_bf, tm):
    (n_a, d), n_b = h1a.shape, h1b.shape[0]
    ta, tb = n_a // tm, n_b // tm
    n = n_a + n_b
    kw = d // 2 // LANES
    return pl.pallas_call(
        functools.partial(_router_kernel, na_tiles=ta),
        grid=(ta + tb,),
        in_specs=[
            pl.BlockSpec((tm, d), lambda i: (jnp.minimum(i, ta - 1), 0)),
            pl.BlockSpec((tm, d), lambda i: (jnp.maximum(i - ta, 0), 0)),
            pl.BlockSpec((1, d), lambda i: (0, 0)),
            pl.BlockSpec((d, LANES), lambda i: (0, 0)),
        ],
        out_specs=[
            pl.BlockSpec((tm * kw, LANES), lambda i: (i, 0)),
            pl.BlockSpec((tm, LANES), lambda i: (i, 0)),
        ],
        out_shape=[
            jax.ShapeDtypeStruct((n * kw, LANES), jnp.uint32),
            jax.ShapeDtypeStruct((n, LANES), F32),
        ],
        compiler_params=_cparams(("arbitrary",)),
        name="router",
    )(h1a, h1b, norm_w.reshape(1, d), wr_bf)


def _route(rt, bm):
    n = rt.shape[0]
    eid = rt[:, :TOP_K].astype(jnp.int32).reshape(-1)
    wts = rt[:, TOP_K:2 * TOP_K].reshape(-1)
    na = n * TOP_K
    onehot = (eid[:, None] == jnp.arange(N_EXPERTS, dtype=jnp.int32)[None, :]).astype(jnp.int32)
    csum = jnp.cumsum(onehot, axis=0)
    rank = jnp.take_along_axis(csum, eid[:, None], axis=1)[:, 0] - 1
    counts = csum[-1]
    pcounts = ((counts + bm - 1) // bm) * bm
    pends = jnp.cumsum(pcounts)
    pstarts = pends - pcounts
    dest = pstarts[eid] + rank
    n_blocks = -(-na // bm) + N_EXPERTS
    rows = n_blocks * bm
    aidx = jnp.arange(na, dtype=jnp.int32)
    inv = jnp.full((rows,), -1, jnp.int32).at[dest].set(aidx)
    valid = inv >= 0
    src = jnp.maximum(inv, 0)
    tok = src // TOP_K
    slot = jnp.where(valid, (src % TOP_K) * n + src // TOP_K,
                     na + jnp.arange(rows, dtype=jnp.int32) % bm)
    wrow = jnp.where(valid, wts[src], 0.0)
    bstart = jnp.arange(n_blocks, dtype=jnp.int32) * bm
    block_e = jnp.minimum(jnp.sum(bstart[:, None] >= pends[None, :], axis=1),
                          N_EXPERTS - 1).astype(jnp.int32)
    nvalid = jnp.clip(pstarts[block_e] + counts[block_e] - bstart, 0, bm).astype(jnp.int32)
    prev_e = jnp.concatenate([jnp.full((1,), -1, jnp.int32), block_e[:-1]])
    first = ((nvalid > 0) & (block_e != prev_e)).astype(jnp.int32)
    wslot = (jnp.cumsum(first) - 1) % 2
    ids = jnp.arange(N_EXPERTS, dtype=jnp.int32)
    later = (ids[None, :] > ids[:, None]) & (counts[None, :] > 0)
    next_nonempty = jnp.min(jnp.where(later, ids[None, :], N_EXPERTS), axis=1)
    next_e = jnp.where(next_nonempty < N_EXPERTS, next_nonempty, -1)[block_e]
    meta = jnp.stack([block_e, nvalid, first, wslot.astype(jnp.int32), next_e.astype(jnp.int32)])
    return meta, tok, slot, wrow.reshape(rows, 1), n_blocks


def _start_row_gather(tok_ref, x_hbm, buf, sem, blk, bm):
    k = buf.shape[0] // bm
    for r in range(bm):
        t = pl.multiple_of(tok_ref[blk * bm + r] * k, k)
        pltpu.make_async_copy(x_hbm.at[pl.ds(t, k)], buf.at[pl.ds(r * k, k)], sem).start(
            priority=r % 2)


def _wait_row_gather(x_hbm, buf, sem, bm):
    k = buf.shape[0] // bm
    for r in range(bm):
        pltpu.make_async_copy(x_hbm.at[pl.ds(0, k)], buf.at[pl.ds(r * k, k)], sem).wait()


META_E, META_NV, META_FIRST, META_WSLOT, META_NEXT = range(5)


def _expert_weights(meta_ref, i, pairs, sem):
    first = meta_ref[META_FIRST, i] == 1
    slot = meta_ref[META_WSLOT, i]
    nxt = meta_ref[META_NEXT, i]

    def copies(e, s):
        return [pltpu.make_async_copy(w.at[e], buf.at[s], sem.at[s]) for w, buf in pairs]

    @pl.when(i == 0)
    def _():
        for c in copies(meta_ref[META_E, 0], 0):
            c.start(priority=1)

    @pl.when(first)
    def _():
        for c in copies(0, slot):
            c.wait()

    @pl.when(first & (nxt >= 0))
    def _():
        for c in copies(nxt, 1 - slot):
            c.start(priority=1)

    return slot


RING = 3


def _gateup_kernel(meta_ref, tok_ref, x_hbm, wg_hbm, wu_hbm, h_ref,
                   xbuf, wg_buf, wu_buf, sem, wsem, *, nb):
    i = pl.program_id(0)
    bm = h_ref.shape[0]
    nv = meta_ref[META_NV, i]
    nv_prev = meta_ref[META_NV, jnp.maximum(i - 1, 0)]
    ws = _expert_weights(meta_ref, i, [(wg_hbm, wg_buf), (wu_hbm, wu_buf)], wsem)

    def start(blk):
        s = blk % RING
        _start_row_gather(tok_ref, x_hbm, xbuf.at[s], sem.at[s], jnp.minimum(blk, nb - 1), bm)

    def wait(blk):
        s = blk % RING
        _wait_row_gather(x_hbm, xbuf.at[s], sem.at[s], bm)

    @pl.when(i == 0)
    def _():
        start(0)
        start(1)

    @pl.when(nv > 0)
    def _():
        wait(i)
        start(i + 2)
        x = _load_packed_rows(xbuf.at[i % RING], bm)
        g = jnp.dot(x, wg_buf[ws], preferred_element_type=F32)
        u = jnp.dot(x, wu_buf[ws], preferred_element_type=F32)
        h_ref[...] = (g * (1.0 / (1.0 + jnp.exp(-g))) * u).astype(BF16)

    @pl.when((nv == 0) & (nv_prev > 0))
    def _():
        wait(i)
        wait(i + 1)

    @pl.when((i == nb - 1) & (nv > 0))
    def _():
        wait(nb)
        wait(nb + 1)

    @pl.when(nv == 0)
    def _():
        h_ref[...] = jnp.zeros_like(h_ref)


def _gateup(meta, tok, hn2p, wg_bf, wu_bf, n_blocks, bm):
    d, f = wg_bf.shape[1], wg_bf.shape[2]
    kw = d // 2 // LANES
    anyspec = pl.BlockSpec(memory_space=pl.ANY)
    grid_spec = pltpu.PrefetchScalarGridSpec(
        num_scalar_prefetch=2,
        grid=(n_blocks,),
        in_specs=[anyspec, anyspec, anyspec],
        out_specs=pl.BlockSpec((bm, f), lambda i, mt, tk: (i, 0)),
        scratch_shapes=[pltpu.VMEM((RING, bm * kw, LANES), jnp.uint32),
                        pltpu.VMEM((2, d, f), BF16), pltpu.VMEM((2, d, f), BF16),
                        pltpu.SemaphoreType.DMA((RING,)), pltpu.SemaphoreType.DMA((2,))],
    )
    return pl.pallas_call(
        functools.partial(_gateup_kernel, nb=n_blocks),
        grid_spec=grid_spec,
        out_shape=jax.ShapeDtypeStruct((n_blocks * bm, f), BF16),
        compiler_params=_cparams(("arbitrary",)),
        name="gateup",
    )(meta, tok, hn2p, wg_bf, wu_bf)


def _start_row_scatter(slot_ref, buf, y_hbm, sem, blk, bm):
    k = buf.shape[0] // bm
    for r in range(bm):
        s = pl.multiple_of(slot_ref[blk * bm + r] * k, k)
        pltpu.make_async_copy(buf.at[pl.ds(r * k, k)], y_hbm.at[pl.ds(s, k)], sem).start(
            priority=r % 2)


def _wait_row_scatter(buf, y_hbm, sem, bm):
    k = buf.shape[0] // bm
    for r in range(bm):
        pltpu.make_async_copy(buf.at[pl.ds(r * k, k)], y_hbm.at[pl.ds(0, k)], sem).wait()


def _down_kernel(meta_ref, slot_ref, h_ref, wd_hbm, w_ref, y_hbm, ybuf, wd_buf, sem, wsem,
                 *, nb):
    i = pl.program_id(0)
    bm = h_ref.shape[0]
    nv = meta_ref[META_NV, i]
    nv_prev = meta_ref[META_NV, jnp.maximum(i - 1, 0)]
    first_empty = (nv == 0) & (nv_prev > 0)
    ws = _expert_weights(meta_ref, i, [(wd_hbm, wd_buf)], wsem)

    def compute():
        y = jnp.dot(h_ref[...], wd_buf[ws], preferred_element_type=F32)
        ybuf[i % RING] = y * w_ref[...]

    def start(blk):
        s = blk % RING
        _start_row_scatter(slot_ref, ybuf.at[s], y_hbm, sem.at[s], blk, bm)

    def wait(blk):
        s = blk % RING
        _wait_row_scatter(ybuf.at[s], y_hbm, sem.at[s], bm)

    @pl.when(i == 0)
    def _():
        rows = ybuf.shape[1]
        ybuf[1] = jnp.zeros(ybuf.shape[1:], F32)
        spare = pltpu.make_async_copy(ybuf.at[1], y_hbm.at[pl.ds(y_hbm.shape[0] - rows, rows)],
                                      sem.at[1])
        spare.start()
        spare.wait()
        compute()

    @pl.when((i >= 3) & ((nv > 0) | first_empty))
    def _():
        wait(i - 3)

    @pl.when((i >= 1) & (nv > 0))
    def _():
        start(i - 1)
        compute()

    @pl.when((i >= 2) & first_empty)
    def _():
        wait(i - 2)

    @pl.when(first_empty)
    def _():
        start(i - 1)
        wait(i - 1)

    @pl.when((i == nb - 1) & (nv > 0))
    def _():
        if nb > 2:
            wait(nb - 3)
        if nb > 1:
            wait(nb - 2)
        start(nb - 1)
        wait(nb - 1)


def _down(meta, slot, h, wd_bf, wrow, n_assign, n_blocks, bm):
    f, d = wd_bf.shape[1], wd_bf.shape[2]
    grid_spec = pltpu.PrefetchScalarGridSpec(
        num_scalar_prefetch=2,
        grid=(n_blocks,),
        in_specs=[
            pl.BlockSpec((bm, f), lambda i, mt, sl: (i, 0)),
            pl.BlockSpec(memory_space=pl.ANY),
            pl.BlockSpec((bm, 1), lambda i, mt, sl: (i, 0)),
        ],
        out_specs=pl.BlockSpec(memory_space=pl.ANY),
        scratch_shapes=[pltpu.VMEM((RING, bm, d), F32),
                        pltpu.VMEM((2, f, d), BF16),
                        pltpu.SemaphoreType.DMA((RING,)), pltpu.SemaphoreType.DMA((2,))],
    )
    return pl.pallas_call(
        functools.partial(_down_kernel, nb=n_blocks),
        grid_spec=grid_spec,
        out_shape=jax.ShapeDtypeStruct((n_assign + bm, d), F32),
        compiler_params=_cparams(("arbitrary",)),
        name="down",
    )(meta, slot, h, wd_bf, wrow)


def _final_kernel(h_ref, y0_ref, y1_ref, nw_ref, o_ref):
    h = h_ref[...] + (y0_ref[...] + y1_ref[...])
    ms = jnp.mean(h * h, axis=-1, keepdims=True)
    o_ref[...] = h * lax.rsqrt(ms + NORM_EPS) * nw_ref[...]


def _final(h1, yc, norm_w, tm, row_off, n_tot):
    n, d = h1.shape
    nt = n // tm
    o0 = row_off // tm
    o1 = (n_tot + row_off) // tm
    return pl.pallas_call(
        _final_kernel,
        grid=(nt,),
        in_specs=[
            pl.BlockSpec((tm, d), lambda i: (i, 0)),
            pl.BlockSpec((tm, d), lambda i: (i + o0, 0)),
            pl.BlockSpec((tm, d), lambda i: (i + o1, 0)),
            pl.BlockSpec((1, d), lambda i: (0, 0)),
        ],
        out_specs=pl.BlockSpec((tm, d), lambda i: (i, 0)),
        out_shape=jax.ShapeDtypeStruct((n, d), F32),
        compiler_params=_cparams(("arbitrary",)),
        name="final",
    )(h1, yc, yc, norm_w.reshape(1, d))


def _tiles(seq, d):
    tm = min(512, seq)
    return dict(tm=tm, tn=min(512, d // 2), tq=min(256, seq), tm_out=tm,
                tn_out=min(1024, d), tm_fin=min(256, seq), bm=min(256, seq))


def _mixer(x, shared, prm, cast_ws):
    b, seq, d = x.shape
    n = b * seq
    t = _tiles(seq, d)
    x2d = x.reshape(n, d)
    qk, v, p = _inproj(x2d, prm["attn_norm_w"], shared["w_in"], shared["rope_real"], seq,
                       t["tm"], t["tn"])
    attn, cast = _attention(shared["lam"], qk, v, shared["qk_meta"], shared["v_meta"],
                            prm["subln_w"], b, seq, t["tq"], cast_ws)
    pool = _pool(p, shared["p_meta"], shared["pool_w"], prm["pool_scale"], seq, t["tm"])
    return _outproj(attn, pool, shared["w_out"], x2d, t["tm_out"], t["tn_out"]), cast


def _moe_and_final(h1a, h1b, shapes, shared, prm):
    seq, d = shapes[0][1], shapes[0][2]
    t = _tiles(seq, d)
    bm = t["bm"]
    n_a, n_b = h1a.shape[0], h1b.shape[0]
    n = n_a + n_b
    hn2, rt = _router(h1a, h1b, prm["ffn_norm_w"], shared["w_router"], t["tm"])
    meta, tok, slot, wrow, n_blocks = _route(rt, bm)
    h = _gateup(meta, tok, hn2, shared["w_gate"], shared["w_up"], n_blocks, bm)
    yc = _down(meta, slot, h, shared["w_down"], wrow, n * TOP_K, n_blocks, bm)
    out_a = _final(h1a, yc, prm["final_norm_w"], t["tm_fin"], 0, n)
    out_b = _final(h1b, yc, prm["final_norm_w"], t["tm_fin"], n_a, n)
    return out_a.reshape(shapes[0]), out_b.reshape(shapes[1])


def kernel(x_prompt, x_sample, meta_tokens, attn_norm_w, w_in, lambda_q1, lambda_k1, lambda_q2, lambda_k2, subln_w, pool_w, pool_scale, w_out, ffn_norm_w, router_group, router_expert, expert_w_gate, expert_w_up, expert_w_down, final_norm_w):
    d = x_prompt.shape[-1]
    seq = x_prompt.shape[1]
    assert x_sample.shape[1] == seq and attn_norm_w.shape[0] == 1
    prm = dict(attn_norm_w=attn_norm_w[0], subln_w=subln_w[0], pool_scale=pool_scale[0],
               ffn_norm_w=ffn_norm_w[0], final_norm_w=final_norm_w)
    wr = jnp.concatenate([router_group[0], router_expert[0]], axis=1)
    wr = jnp.pad(wr, ((0, 0), (0, LANES - wr.shape[1])))
    t = _tiles(seq, d)
    shared = dict(
        w_in=_prep_w_in(w_in[0], t["tn"]), w_out=w_out[0].astype(BF16), pool_w=pool_w[0].astype(BF16),
        w_router=wr.astype(BF16),
        lam=_lam(lambda_q1[0], lambda_k1[0], lambda_q2[0], lambda_k2[0]),
        rope_real=_rope_tables(jnp.arange(N_META, N_META + seq, dtype=F32)),
    )
    meta_pad = jnp.pad(meta_tokens.astype(F32), ((0, META_ROWS - N_META), (0, 0)))
    qk_m, v_m, p_m = _inproj(meta_pad, prm["attn_norm_w"], shared["w_in"],
                             _rope_tables(jnp.arange(META_ROWS, dtype=F32)), META_ROWS,
                             META_ROWS, t["tn"])
    shared.update(qk_meta=qk_m, v_meta=v_m, p_meta=p_m)
    h1a, (w_down_bf,) = _mixer(x_prompt, shared, prm, (expert_w_down[0],))
    h1b, (w_gate_bf, w_up_bf) = _mixer(x_sample, shared, prm,
                                       (expert_w_gate[0], expert_w_up[0]))
    shared.update(w_gate=w_gate_bf, w_up=w_up_bf, w_down=w_down_bf)
    return _moe_and_final(h1a, h1b, (x_prompt.shape, x_sample.shape), shared, prm)
```

```python
import functools
import math

import jax
import jax.numpy as jnp
from jax import lax
from jax.experimental import pallas as pl
from jax.experimental.pallas import tpu as pltpu

F32 = jnp.float32
BF16 = jnp.bfloat16

N_META = 16
HEAD_DIM = 64
V_DIM = 2 * HEAD_DIM
ROT_DIM = HEAD_DIM // 4
ROPE_THETA = 500000.0
ATTN_SCALE = HEAD_DIM ** -0.5
SUBLN_EPS = 1e-5
NORM_EPS = 1e-6
POOL_WINDOWS = (2, 4, 8, 16)
N_GROUPS = 4
EXPERTS_PER_GROUP = 8
N_EXPERTS = N_GROUPS * EXPERTS_PER_GROUP
TOP_K = 2
LAMBDA_INIT = 0.8 - 0.6 * math.exp(-0.3 * 0)

LANES = 128
SUBLANES = 8
META_ROWS = 128
HALO = 8
MASK_VALUE = -1e30

VMEM_LIMIT = 56 * 1024 * 1024


def _cparams(sem):
    return pltpu.CompilerParams(dimension_semantics=sem, vmem_limit_bytes=VMEM_LIMIT)


def _inproj_kernel(x_ref, nw_ref, w_ref, c_ref, s_ref,
                   qk_ref, v_ref, p_ref, hn_ref, acc_a, acc_b, *, nqk, nv, ncol, nsteps):
    t = pl.program_id(0)
    j = t % ncol
    jp = jnp.maximum(t - 1, 0) % ncol

    @pl.when(t == 0)
    def _():
        acc_b[...] = jnp.zeros_like(acc_b)

    @pl.when((j == 0) & (t < nsteps))
    def _():
        x = x_ref[...]
        ms = jnp.mean(x * x, axis=-1, keepdims=True)
        hn_ref[...] = (x * lax.rsqrt(ms + NORM_EPS) * nw_ref[...]).astype(BF16)

    def matmul(acc_w):
        acc_w[...] = jnp.dot(hn_ref[...], w_ref[...], preferred_element_type=F32)

    def epilogue_qk(acc_r):
        scale = jnp.where(jp < nqk // 2, ATTN_SCALE, 1.0).astype(F32)
        c = c_ref[...] * scale
        s = s_ref[...] * scale
        for u in range(acc_r.shape[1] // LANES):
            blk = acc_r[:, u * LANES:(u + 1) * LANES]
            r = blk * c + pltpu.roll(blk, LANES // 2, 1) * s
            qk_ref[:, u * LANES:(u + 1) * LANES] = r.astype(BF16)

    def epilogue_v(acc_r):
        lane = lax.broadcasted_iota(jnp.int32, (acc_r.shape[0], LANES), 1)
        ones_col = jnp.where(lane == 0, 1.0, 0.0).astype(BF16)
        for u in range(acc_r.shape[1] // LANES):
            v_ref[:, 2 * u * LANES:(2 * u + 1) * LANES] = acc_r[:, u * LANES:(u + 1) * LANES].astype(BF16)
            v_ref[:, (2 * u + 1) * LANES:(2 * u + 2) * LANES] = ones_col

    def epilogue_p(acc_r):
        p_ref[...] = acc_r[...]

    sections = ((jp < nqk, epilogue_qk),
                ((jp >= nqk) & (jp < nqk + nv), epilogue_v),
                (jp >= nqk + nv, epilogue_p))
    for parity, (acc_w, acc_r) in enumerate(((acc_a, acc_b), (acc_b, acc_a))):
        for in_section, epilogue in sections:
            @pl.when((t % 2 == parity) & in_section)
            def _(acc_w=acc_w, acc_r=acc_r, epilogue=epilogue):
                epilogue(acc_r)
                matmul(acc_w)


_HALF = ROT_DIM // 2
HEAD_LANE_SRC = (list(range(0, _HALF)) + list(range(HEAD_DIM, HEAD_DIM + _HALF))
                 + list(range(ROT_DIM, HEAD_DIM))
                 + list(range(_HALF, ROT_DIM)) + list(range(HEAD_DIM + _HALF, HEAD_DIM + ROT_DIM))
                 + list(range(HEAD_DIM + ROT_DIM, 2 * HEAD_DIM)))
COMPONENT0_LANES = tuple(i for i, src in enumerate(HEAD_LANE_SRC) if src < HEAD_DIM)


def _prep_w_in_kernel(w_ref, perm_ref, o_ref, *, nqk):
    j = pl.program_id(1)

    @pl.when(j < nqk)
    def _():
        for t in range(w_ref.shape[1] // LANES):
            sl = slice(t * LANES, (t + 1) * LANES)
            o_ref[:, sl] = jnp.dot(w_ref[:, sl].astype(BF16), perm_ref[...],
                                   preferred_element_type=F32).astype(BF16)

    @pl.when(j >= nqk)
    def _():
        o_ref[...] = w_ref[...].astype(BF16)


def _prep_w_in(w_in, tn):
    d, d_in = w_in.shape
    nqk = d // tn
    perm = jnp.zeros((V_DIM, V_DIM), F32).at[jnp.array(HEAD_LANE_SRC), jnp.arange(V_DIM)].set(1.0)
    tr = min(1024, d)
    return pl.pallas_call(
        functools.partial(_prep_w_in_kernel, nqk=nqk),
        grid=(d // tr, d_in // tn),
        in_specs=[pl.BlockSpec((tr, tn), lambda i, j: (i, j)),
                  pl.BlockSpec((V_DIM, V_DIM), lambda i, j: (0, 0))],
        out_specs=pl.BlockSpec((tr, tn), lambda i, j: (i, j)),
        out_shape=jax.ShapeDtypeStruct((d, d_in), BF16),
        compiler_params=_cparams(("arbitrary", "arbitrary")),
        name="prep_w_in",
    )(w_in, perm.astype(BF16))


def _rope_tables(pos):
    inv_freq = ROPE_THETA ** (-jnp.arange(0, ROT_DIM, 2, dtype=F32) / ROT_DIM)
    ang = pos[:, None] * inv_freq[None, :]
    cos, sin = jnp.cos(ang), jnp.sin(ang)
    r = pos.shape[0]
    ones = jnp.ones((r, HEAD_DIM - ROT_DIM), F32)
    zeros = jnp.zeros((r, HEAD_DIM - ROT_DIM), F32)
    c = jnp.concatenate([cos, cos, ones, cos, cos, ones], axis=1)
    s = jnp.concatenate([-sin, -sin, zeros, sin, sin, zeros], axis=1)
    return c, s


def _inproj(x2d, norm_w, w_in_bf, tables, rows_per_seq, tm, tn):
    n, d = x2d.shape
    d_in = w_in_bf.shape[1]
    attn_w = d // 2
    nqk = 2 * attn_w // tn
    nv = attn_w // tn
    npool = (d_in - 3 * attn_w) // tn
    tps = rows_per_seq // tm
    c, s = tables
    nrow, ncol = n // tm, d_in // tn
    nsteps = nrow * ncol

    def prev(t):
        tp = jnp.maximum(t - 1, 0)
        return tp // ncol, tp % ncol

    tab_spec = pl.BlockSpec((tm, LANES), lambda t: (prev(t)[0] % tps, 0))
    kern = functools.partial(_inproj_kernel, nqk=nqk, nv=nv, ncol=ncol, nsteps=nsteps)
    return pl.pallas_call(
        kern,
        grid=(nsteps + 1,),
        in_specs=[
            pl.BlockSpec((tm, d), lambda t: (jnp.minimum(t // ncol, nrow - 1), 0)),
            pl.BlockSpec((1, d), lambda t: (0, 0)),
            pl.BlockSpec((d, tn), lambda t: (0, t % ncol)),
            tab_spec, tab_spec,
        ],
        out_specs=[
            pl.BlockSpec((tm, tn), lambda t: (prev(t)[0], jnp.minimum(prev(t)[1], nqk - 1))),
            pl.BlockSpec((tm, 2 * tn),
                         lambda t: (prev(t)[0], jnp.clip(prev(t)[1] - nqk, 0, nv - 1))),
            pl.BlockSpec((tm, tn),
                         lambda t: (prev(t)[0], jnp.clip(prev(t)[1] - nqk - nv, 0, npool - 1))),
        ],
        out_shape=[
            jax.ShapeDtypeStruct((n, 2 * attn_w), BF16),
            jax.ShapeDtypeStruct((n, 2 * attn_w), BF16),
            jax.ShapeDtypeStruct((n, d_in - 3 * attn_w), F32),
        ],
        scratch_shapes=[pltpu.VMEM((tm, d), BF16), pltpu.VMEM((tm, tn), F32),
                        pltpu.VMEM((tm, tn), F32)],
        compiler_params=_cparams(("arbitrary",)),
        name="inproj",
    )(x2d, norm_w.reshape(1, d), w_in_bf, c, s)


def _lam_kernel(q1_ref, k1_ref, q2_ref, k2_ref, o_ref):
    s1 = jnp.sum(q1_ref[...] * k1_ref[...], axis=-1, keepdims=True)
    s2 = jnp.sum(q2_ref[...] * k2_ref[...], axis=-1, keepdims=True)
    o_ref[...] = jnp.exp(s1) - jnp.exp(s2) + LAMBDA_INIT


def _lam(lq1, lk1, lq2, lk2):
    r = lambda t: t.reshape(1, -1).astype(F32)
    return pl.pallas_call(
        _lam_kernel,
        out_shape=jax.ShapeDtypeStruct((1, 1), F32),
        name="lam",
    )(r(lq1), r(lk1), r(lq2), r(lk2))


def _attn_step(refs, P, Q):
    lam_ref, q_ref, k_ref, km_ref, v_ref, vm_ref, sw_ref, o_ref, cast_in, cast_out = refs
    tq = q_ref.shape[0]

    for w_in, w_out in zip(cast_in, cast_out):
        w_out[...] = w_in[...].astype(BF16)

    oe = P["oe"][...]
    r = 1.0 / oe[:, V_DIM:V_DIM + 1]
    on = oe[:, :V_DIM] * r
    o = on[:tq] - lam_ref[...] * on[tq:]
    ms = jnp.mean(o * o, axis=-1, keepdims=True)
    y = o * lax.rsqrt(ms + SUBLN_EPS) * sw_ref[...]
    o_ref[...] = (y * (1.0 - LAMBDA_INIT)).astype(BF16)

    m_p = Q["m"][:, :1]
    e = jnp.exp(Q["s"][...] - m_p).astype(BF16)
    em = jnp.exp(Q["sm"][...] - m_p).astype(BF16)
    Q["oe"][...] = (jnp.dot(e, v_ref[...], preferred_element_type=F32)
                    + jnp.dot(em, vm_ref[...], preferred_element_type=F32))

    q = q_ref[...]
    lane = lax.broadcasted_iota(jnp.int32, q.shape, 1)
    zero = jnp.zeros_like(q)
    assert COMPONENT0_LANES == tuple(range(0, 8)) + tuple(range(16, 72))
    comp0 = (lane < 8) | ((lane >= 16) & (lane < 72))
    qs = jnp.concatenate([jnp.where(comp0, q, zero),
                          jnp.where(comp0, zero, q)], axis=0)
    dn = (((1,), (1,)), ((), ()))
    s = lax.dot_general(qs, k_ref[...], dn, preferred_element_type=F32)
    sm = lax.dot_general(qs, km_ref[...], dn, preferred_element_type=F32)
    col = lax.broadcasted_iota(jnp.int32, sm.shape, 1)
    sm = jnp.where(col < N_META, sm, MASK_VALUE)
    m = jnp.maximum(jnp.max(s, axis=-1, keepdims=True), jnp.max(sm, axis=-1, keepdims=True))
    P["s"][...] = s
    P["sm"][...] = sm
    P["m"][...] = jnp.broadcast_to(m, P["m"].shape)


ATTN_SCRATCH = ("s", "sm", "m", "oe")


def _attn_kernel(*refs, n_cast):
    lam_ref, q_ref, k_ref, km_ref, v_ref, vm_ref, sw_ref = refs[:7]
    cast_in = refs[7:7 + n_cast]
    o_ref = refs[7 + n_cast]
    cast_out = refs[8 + n_cast:8 + 2 * n_cast]
    scratch = refs[8 + 2 * n_cast:]
    A = dict(zip(ATTN_SCRATCH, scratch[0::2]))
    B = dict(zip(ATTN_SCRATCH, scratch[1::2]))
    t = pl.program_id(0)
    args = (lam_ref, q_ref, k_ref, km_ref, v_ref, vm_ref, sw_ref, o_ref, cast_in, cast_out)

    @pl.when(t == 0)
    def _():
        for name in ("s", "sm", "m"):
            B[name][...] = jnp.zeros_like(B[name])
        A["oe"][...] = jnp.ones_like(A["oe"])

    @pl.when(t % 2 == 0)
    def _():
        _attn_step(args, A, B)

    @pl.when(t % 2 == 1)
    def _():
        _attn_step(args, B, A)


def _attention(lam, qk, vx, qk_meta, vx_meta, subln_w, batch, seq, tq, cast_ws=()):
    n, two_w = qk.shape
    n_heads = two_w // 2 // V_DIM
    nqt = seq // tq
    units = batch * n_heads * nqt
    vw = 2 * V_DIM

    def unit(u):
        bh = u // nqt
        return bh // n_heads, bh % n_heads, u % nqt

    def cur(t):
        return unit(jnp.minimum(t, units - 1))

    def prv(t):
        return unit(jnp.clip(t - 1, 0, units - 1))

    def prv2(t):
        return unit(jnp.maximum(t - 2, 0))

    def q_map(t):
        b, h, i = cur(t)
        return (b * nqt + i, h)

    def k_map(t):
        b, h, _ = cur(t)
        return (b, n_heads + h)

    def km_map(t):
        _, h, _ = cur(t)
        return (0, n_heads + h)

    def v_map(t):
        b, h, _ = prv(t)
        return (b, h)

    def vm_map(t):
        _, h, _ = prv(t)
        return (0, h)

    def o_map(t):
        b, h, i = prv2(t)
        return (b * nqt + i, h)

    steps = units + 2
    n_cast = len(cast_ws)
    cast_2d, cast_specs, cast_shapes = [], [], []
    if n_cast:
        cast_blocks = 1 << (steps.bit_length() - 1)
        for w in cast_ws:
            w2 = w.reshape(-1, w.shape[-1])
            rows = w2.shape[0] // cast_blocks
            assert rows * cast_blocks == w2.shape[0] and rows % 16 == 0, (w.shape, cast_blocks)
            spec = pl.BlockSpec((rows, w2.shape[1]),
                                lambda t: (jnp.minimum(t, cast_blocks - 1), 0))
            cast_2d.append(w2)
            cast_specs.append(spec)
            cast_shapes.append(jax.ShapeDtypeStruct(w2.shape, BF16))

    outs = pl.pallas_call(
        functools.partial(_attn_kernel, n_cast=n_cast),
        grid=(steps,),
        in_specs=[
            pl.BlockSpec((1, 1), lambda t: (0, 0)),
            pl.BlockSpec((tq, V_DIM), q_map),
            pl.BlockSpec((seq, V_DIM), k_map),
            pl.BlockSpec((META_ROWS, V_DIM), km_map),
            pl.BlockSpec((seq, vw), v_map),
            pl.BlockSpec((META_ROWS, vw), vm_map),
            pl.BlockSpec((1, V_DIM), lambda t: (0, 0)),
        ] + cast_specs,
        out_specs=[pl.BlockSpec((tq, V_DIM), o_map)] + cast_specs,
        out_shape=[jax.ShapeDtypeStruct((n, two_w // 2), BF16)] + cast_shapes,
        scratch_shapes=(
            [pltpu.VMEM((2 * tq, seq), F32)] * 2
            + [pltpu.VMEM((2 * tq, META_ROWS), F32)] * 2
            + [pltpu.VMEM((2 * tq, LANES), F32)] * 2
            + [pltpu.VMEM((2 * tq, vw), F32)] * 2
        ),
        compiler_params=_cparams(("arbitrary",)),
        name="attn",
    )(lam, qk, qk, qk_meta, vx, vx_meta, subln_w.reshape(1, V_DIM).astype(F32), *cast_2d)
    return outs[0], [o.reshape(w.shape) for o, w in zip(outs[1:], cast_ws)]


def _pool_kernel(p_ref, prev_ref, next_ref, pm_ref, w_ref, sc_ref, o_ref, *, tps, total_t):
    i = pl.program_id(0)
    tm = p_ref.shape[0]
    ti = i % tps
    p = p_ref[...]
    prev = jnp.where(ti == 0, pm_ref[...], prev_ref[...])
    nxt = jnp.where(ti == tps - 1, 0.0, next_ref[...])
    ext = jnp.concatenate([prev, p, nxt], axis=0)
    rows = tm + 2 * HALO
    shift = lambda t, k: pltpu.roll(t, rows - k, 0)
    t_pos = N_META + ti * tm + lax.broadcasted_iota(jnp.int32, (tm, 1), 0)
    pch = p.shape[1] // len(POOL_WINDOWS)
    for g, w in enumerate(POOL_WINDOWS):
        sl = slice(g * pch, (g + 1) * pch)
        acc = ext[:, sl]
        span = 1
        while span < w:
            acc = acc + shift(acc, span)
            span *= 2
        off = HALO - w // 2
        win = (shift(acc, off) if off else acc)[:tm]
        cnt = jnp.minimum(w, total_t - t_pos + w // 2).astype(F32)
        z = (win / cnt - p[:, sl]).astype(BF16)
        y = jnp.dot(z, w_ref[g], preferred_element_type=F32)
        o_ref[:, sl] = (y * sc_ref[:, sl]).astype(BF16)


def _pool(p, p_meta, pool_w_bf, pool_scale, seq, tm):
    n, pw = p.shape
    tps = seq // tm
    hb = tm // HALO
    nb = n // HALO
    kern = functools.partial(_pool_kernel, tps=tps, total_t=seq + N_META)
    return pl.pallas_call(
        kern,
        grid=(n // tm,),
        in_specs=[
            pl.BlockSpec((tm, pw), lambda i: (i, 0)),
            pl.BlockSpec((HALO, pw), lambda i: (jnp.maximum(i * hb - 1, 0), 0)),
            pl.BlockSpec((HALO, pw), lambda i: (jnp.minimum((i + 1) * hb, nb - 1), 0)),
            pl.BlockSpec((HALO, pw), lambda i: (N_META // HALO - 1, 0)),
            pl.BlockSpec(pool_w_bf.shape, lambda i: (0, 0, 0)),
            pl.BlockSpec((1, pw), lambda i: (0, 0)),
        ],
        out_specs=pl.BlockSpec((tm, pw), lambda i: (i, 0)),
        out_shape=jax.ShapeDtypeStruct((n, pw), BF16),
        compiler_params=_cparams(("arbitrary",)),
        name="pool",
    )(p, p, p, p_meta, pool_w_bf, pool_scale.reshape(1, pw).astype(F32))


def _outproj_kernel(a_ref, p_ref, wa_ref, wp_ref, x_ref, o_ref):
    acc = jnp.dot(a_ref[...], wa_ref[...], preferred_element_type=F32)
    acc = acc + jnp.dot(p_ref[...], wp_ref[...], preferred_element_type=F32)
    o_ref[...] = x_ref[...] + acc


def _outproj(attn, pool, w_out_bf, x2d, tm, tn):
    n, d = x2d.shape
    half = attn.shape[1]
    return pl.pallas_call(
        _outproj_kernel,
        grid=(n // tm, d // tn),
        in_specs=[
            pl.BlockSpec((tm, half), lambda i, j: (i, 0)),
            pl.BlockSpec((tm, half), lambda i, j: (i, 0)),
            pl.BlockSpec((half, tn), lambda i, j: (0, j)),
            pl.BlockSpec((half, tn), lambda i, j: (1, j)),
            pl.BlockSpec((tm, tn), lambda i, j: (i, j)),
        ],
        out_specs=pl.BlockSpec((tm, tn), lambda i, j: (i, j)),
        out_shape=jax.ShapeDtypeStruct((n, d), F32),
        compiler_params=_cparams(("arbitrary", "arbitrary")),
        name="outproj",
    )(attn, pool, w_out_bf, w_out_bf, x2d)


def _route_rows(lg):
    lane = lax.broadcasted_iota(jnp.int32, lg.shape, 1).astype(F32)
    ninf = -jnp.inf
    nolane = float(LANES)
    rmax = lambda t: jnp.max(t, axis=-1, keepdims=True)
    rmin = lambda t: jnp.min(t, axis=-1, keepdims=True)
    is_g = lane < N_GROUPS
    gmax = rmax(jnp.where(is_g, lg, ninf))
    gsel = rmin(jnp.where(is_g & (lg == gmax), lane, nolane))
    p_g = 1.0 / jnp.sum(jnp.where(is_g, jnp.exp(lg - gmax), 0.0), axis=-1, keepdims=True)
    lo = N_GROUPS + EXPERTS_PER_GROUP * gsel
    in_e = (lane >= lo) & (lane < lo + EXPERTS_PER_GROUP)
    v1 = rmax(jnp.where(in_e, lg, ninf))
    i1 = rmin(jnp.where(in_e & (lg == v1), lane, nolane))
    in_e2 = in_e & (lane != i1)
    v2 = rmax(jnp.where(in_e2, lg, ninf))
    i2 = rmin(jnp.where(in_e2 & (lg == v2), lane, nolane))
    t = jnp.exp(v2 - v1)
    w1 = p_g / (1.0 + t)
    w2 = w1 * t
    out = jnp.where(lane == 0, i1 - N_GROUPS, 0.0)
    out = jnp.where(lane == 1, i2 - N_GROUPS, out)
    out = jnp.where(lane == 2, w1, out)
    return jnp.where(lane == 3, w2, out)


def _store_token_major(ref, x):
    rows, k = x.shape[0], x.shape[1] // LANES
    for c in range(k):
        ref[pl.ds(c, rows, stride=k), :] = x[:, c * LANES:(c + 1) * LANES]


def _load_token_major(ref, rows):
    k = ref.shape[0] // rows
    return [ref[pl.ds(c, rows, stride=k), :] for c in range(k)]


def _store_packed_rows(ref, x_bf):
    half = x_bf.shape[1] // 2
    lo = pltpu.bitcast(x_bf[:, :half].astype(F32), jnp.uint32)
    hi = pltpu.bitcast(x_bf[:, half:].astype(F32), jnp.uint32)
    _store_token_major(ref, (lo >> 16) | (hi & jnp.uint32(0xFFFF0000)))


def _load_packed_rows(ref, rows):
    words = _load_token_major(ref, rows)
    lo = [pltpu.bitcast(w << 16, F32).astype(BF16) for w in words]
    hi = [pltpu.bitcast(w & jnp.uint32(0xFFFF0000), F32).astype(BF16) for w in words]
    return jnp.concatenate(lo + hi, axis=1)


def _router_kernel(ha_ref, hb_ref, nw_ref, wr_ref, hn_ref, rt_ref, *, na_tiles):
    i = pl.program_id(0)

    def body(h_ref):
        h = h_ref[...]
        ms = jnp.mean(h * h, axis=-1, keepdims=True)
        hn = (h * lax.rsqrt(ms + NORM_EPS) * nw_ref[...]).astype(BF16)
        _store_packed_rows(hn_ref, hn)
        lg = jnp.dot(hn, wr_ref[...], preferred_element_type=F32)
        rt_ref[...] = _route_rows(lg)

    @pl.when(i < na_tiles)
    def _():
        body(ha_ref)

    @pl.when(i >= na_tiles)
    def _():
        body(hb_ref)


def _router(h1a, h1b, norm_w, wr_bf, tm):
    (n_a, d), n_b = h1a.shape, h1b.shape[0]
    ta, tb = n_a // tm, n_b // tm
    n = n_a + n_b
    kw = d // 2 // LANES
    return pl.pallas_call(
        functools.partial(_router_kernel, na_tiles=ta),
        grid=(ta + tb,),
        in_specs=[
            pl.BlockSpec((tm, d), lambda i: (jnp.minimum(i, ta - 1), 0)),
            pl.BlockSpec((tm, d), lambda i: (jnp.maximum(i - ta, 0), 0)),
            pl.BlockSpec((1, d), lambda i: (0, 0)),
            pl.BlockSpec((d, LANES), lambda i: (0, 0)),
        ],
        out_specs=[
            pl.BlockSpec((tm * kw, LANES), lambda i: (i, 0)),
            pl.BlockSpec((tm, LANES), lambda i: (i, 0)),
        ],
        out_shape=[
            jax.ShapeDtypeStruct((n * kw, LANES), jnp.uint32),
            jax.ShapeDtypeStruct((n, LANES), F32),
        ],
        compiler_params=_cparams(("arbitrary",)),
        name="router",
    )(h1a, h1b, norm_w.reshape(1, d), wr_bf)


def _route(rt, bm):
    n = rt.shape[0]
    eid = rt[:, :TOP_K].astype(jnp.int32).reshape(-1)
    wts = rt[:, TOP_K:2 * TOP_K].reshape(-1)
    na = n * TOP_K
    ids = jnp.arange(N_EXPERTS, dtype=jnp.int32)
    counts = jnp.sum((eid[:, None] == ids[None, :]).astype(jnp.int32), axis=0)
    order = jnp.argsort(eid, stable=True).astype(jnp.int32)
    starts = jnp.cumsum(counts) - counts
    pcounts = ((counts + bm - 1) // bm) * bm
    pends = jnp.cumsum(pcounts)
    pstarts = pends - pcounts
    n_blocks = -(-na // bm) + N_EXPERTS
    rows = n_blocks * bm
    bstart = jnp.arange(n_blocks, dtype=jnp.int32) * bm
    block_e = jnp.minimum(jnp.sum(bstart[:, None] >= pends[None, :], axis=1),
                          N_EXPERTS - 1).astype(jnp.int32)
    nvalid = jnp.clip(pstarts[block_e] + counts[block_e] - bstart, 0, bm).astype(jnp.int32)
    r_in = jnp.arange(bm, dtype=jnp.int32)[None, :]
    valid = (r_in < nvalid[:, None]).reshape(-1)
    pos = (starts[block_e] + bstart - pstarts[block_e])[:, None] + r_in
    src = jnp.where(valid, order[jnp.clip(pos, 0, na - 1).reshape(-1)], 0)
    tok = src // TOP_K
    slot = jnp.where(valid, (src % TOP_K) * n + src // TOP_K,
                     na + jnp.arange(rows, dtype=jnp.int32) % bm)
    wrow = jnp.where(valid, wts[src], 0.0)
    prev_e = jnp.concatenate([jnp.full((1,), -1, jnp.int32), block_e[:-1]])
    first = ((nvalid > 0) & (block_e != prev_e)).astype(jnp.int32)
    wslot = (jnp.cumsum(first) - 1) % 2
    later = (ids[None, :] > ids[:, None]) & (counts[None, :] > 0)
    next_nonempty = jnp.min(jnp.where(later, ids[None, :], N_EXPERTS), axis=1)
    next_e = jnp.where(next_nonempty < N_EXPERTS, next_nonempty, -1)[block_e]
    meta = jnp.stack([block_e, nvalid, first, wslot.astype(jnp.int32), next_e.astype(jnp.int32)])
    return meta, tok, slot, wrow.reshape(rows, 1), n_blocks


def _start_row_gather(tok_ref, x_hbm, buf, sem, blk, bm, part=0, parts=1):
    k = buf.shape[0] // bm
    for r in range(part * bm // parts, (part + 1) * bm // parts):
        t = pl.multiple_of(tok_ref[blk * bm + r] * k, k)
        pltpu.make_async_copy(x_hbm.at[pl.ds(t, k)], buf.at[pl.ds(r * k, k)], sem).start(
            priority=r % 2)


def _wait_row_gather(x_hbm, buf, sem, bm):
    k = buf.shape[0] // bm
    for r in range(bm):
        pltpu.make_async_copy(x_hbm.at[pl.ds(0, k)], buf.at[pl.ds(r * k, k)], sem).wait()


META_E, META_NV, META_FIRST, META_WSLOT, META_NEXT = range(5)


def _expert_weights(meta_ref, i, pairs, sem):
    first = meta_ref[META_FIRST, i] == 1
    slot = meta_ref[META_WSLOT, i]
    nxt = meta_ref[META_NEXT, i]

    def copies(e, s):
        return [pltpu.make_async_copy(w.at[e], buf.at[s], sem.at[s]) for w, buf in pairs]

    @pl.when(i == 0)
    def _():
        for c in copies(meta_ref[META_E, 0], 0):
            c.start(priority=1)

    @pl.when(first)
    def _():
        for c in copies(0, slot):
            c.wait()

    @pl.when(first & (nxt >= 0))
    def _():
        for c in copies(nxt, 1 - slot):
            c.start(priority=1)

    return slot


RING = 3
GATEUP_CHUNKS = 2


def _gateup_kernel(meta_ref, tok_ref, x_hbm, wg_hbm, wu_hbm, h_ref,
                   xbuf, wg_buf, wu_buf, sem, wsem, *, nb):
    i = pl.program_id(0)
    bm = h_ref.shape[0]
    nv = meta_ref[META_NV, i]
    nv_prev = meta_ref[META_NV, jnp.maximum(i - 1, 0)]
    ws = _expert_weights(meta_ref, i, [(wg_hbm, wg_buf), (wu_hbm, wu_buf)], wsem)

    def start(blk, part=0, parts=1):
        s = blk % RING
        _start_row_gather(tok_ref, x_hbm, xbuf.at[s], sem.at[s], jnp.minimum(blk, nb - 1), bm,
                          part, parts)

    def wait(blk):
        s = blk % RING
        _wait_row_gather(x_hbm, xbuf.at[s], sem.at[s], bm)

    @pl.when(i == 0)
    def _():
        start(0)
        start(1)

    @pl.when(nv > 0)
    def _():
        wait(i)
        x = _load_packed_rows(xbuf.at[i % RING], bm)
        f = h_ref.shape[1]
        cw = f // GATEUP_CHUNKS
        for c in range(GATEUP_CHUNKS):
            cols = slice(c * cw, (c + 1) * cw)
            g = jnp.dot(x, wg_buf[ws, :, cols], preferred_element_type=F32)
            start(i + 2, 2 * c, 2 * GATEUP_CHUNKS)
            u = jnp.dot(x, wu_buf[ws, :, cols], preferred_element_type=F32)
            start(i + 2, 2 * c + 1, 2 * GATEUP_CHUNKS)
            h_ref[:, cols] = (g * (1.0 / (1.0 + jnp.exp(-g))) * u).astype(BF16)

    @pl.when((nv == 0) & (nv_prev > 0))
    def _():
        wait(i)
        wait(i + 1)

    @pl.when((i == nb - 1) & (nv > 0))
    def _():
        wait(nb)
        wait(nb + 1)

    @pl.when(nv == 0)
    def _():
        h_ref[...] = jnp.zeros_like(h_ref)


def _gateup(meta, tok, hn2p, wg_bf, wu_bf, n_blocks, bm):
    d, f = wg_bf.shape[1], wg_bf.shape[2]
    kw = d // 2 // LANES
    anyspec = pl.BlockSpec(memory_space=pl.ANY)
    grid_spec = pltpu.PrefetchScalarGridSpec(
        num_scalar_prefetch=2,
        grid=(n_blocks,),
        in_specs=[anyspec, anyspec, anyspec],
        out_specs=pl.BlockSpec((bm, f), lambda i, mt, tk: (i, 0)),
        scratch_shapes=[pltpu.VMEM((RING, bm * kw, LANES), jnp.uint32),
                        pltpu.VMEM((2, d, f), BF16), pltpu.VMEM((2, d, f), BF16),
                        pltpu.SemaphoreType.DMA((RING,)), pltpu.SemaphoreType.DMA((2,))],
    )
    return pl.pallas_call(
        functools.partial(_gateup_kernel, nb=n_blocks),
        grid_spec=grid_spec,
        out_shape=jax.ShapeDtypeStruct((n_blocks * bm, f), BF16),
        compiler_params=_cparams(("arbitrary",)),
        name="gateup",
    )(meta, tok, hn2p, wg_bf, wu_bf)


def _start_row_scatter(slot_ref, buf, y_hbm, sem, blk, bm, part=0, parts=1):
    k = buf.shape[0] // bm
    for r in range(part * bm // parts, (part + 1) * bm // parts):
        s = pl.multiple_of(slot_ref[blk * bm + r] * k, k)
        pltpu.make_async_copy(buf.at[pl.ds(r * k, k)], y_hbm.at[pl.ds(s, k)], sem).start(
            priority=r % 2)


def _wait_row_scatter(buf, y_hbm, sem, bm):
    k = buf.shape[0] // bm
    for r in range(bm):
        pltpu.make_async_copy(buf.at[pl.ds(r * k, k)], y_hbm.at[pl.ds(0, k)], sem).wait()


DOWN_CHUNKS = 4


def _down_kernel(meta_ref, slot_ref, h_ref, wd_hbm, w_ref, y_hbm, ybuf, wd_buf, sem, wsem,
                 *, nb):
    i = pl.program_id(0)
    bm = h_ref.shape[0]
    nv = meta_ref[META_NV, i]
    nv_prev = meta_ref[META_NV, jnp.maximum(i - 1, 0)]
    first_empty = (nv == 0) & (nv_prev > 0)
    ws = _expert_weights(meta_ref, i, [(wd_hbm, wd_buf)], wsem)

    def compute(between=None):
        d = wd_buf.shape[2]
        cw = d // DOWN_CHUNKS
        for c in range(DOWN_CHUNKS):
            cols = slice(c * cw, (c + 1) * cw)
            y = jnp.dot(h_ref[...], wd_buf[ws, :, cols], preferred_element_type=F32)
            ybuf[i % RING, :, cols] = y * w_ref[...]
            if between is not None:
                between(c)

    def start(blk, part=0, parts=1):
        s = blk % RING
        _start_row_scatter(slot_ref, ybuf.at[s], y_hbm, sem.at[s], blk, bm, part, parts)

    def wait(blk):
        s = blk % RING
        _wait_row_scatter(ybuf.at[s], y_hbm, sem.at[s], bm)

    @pl.when(i == 0)
    def _():
        rows = ybuf.shape[1]
        ybuf[1] = jnp.zeros(ybuf.shape[1:], F32)
        spare = pltpu.make_async_copy(ybuf.at[1], y_hbm.at[pl.ds(y_hbm.shape[0] - rows, rows)],
                                      sem.at[1])
        spare.start()
        spare.wait()
        compute()

    @pl.when((i >= 3) & ((nv > 0) | first_empty))
    def _():
        wait(i - 3)

    @pl.when((i >= 1) & (nv > 0))
    def _():
        compute(lambda c: start(i - 1, c, DOWN_CHUNKS))

    @pl.when((i >= 2) & first_empty)
    def _():
        wait(i - 2)

    @pl.when(first_empty)
    def _():
        start(i - 1)
        wait(i - 1)

    @pl.when((i == nb - 1) & (nv > 0))
    def _():
        if nb > 2:
            wait(nb - 3)
        if nb > 1:
            wait(nb - 2)
        start(nb - 1)
        wait(nb - 1)


def _down(meta, slot, h, wd_bf, wrow, n_assign, n_blocks, bm):
    f, d = wd_bf.shape[1], wd_bf.shape[2]
    grid_spec = pltpu.PrefetchScalarGridSpec(
        num_scalar_prefetch=2,
        grid=(n_blocks,),
        in_specs=[
            pl.BlockSpec((bm, f), lambda i, mt, sl: (i, 0)),
            pl.BlockSpec(memory_space=pl.ANY),
            pl.BlockSpec((bm, 1), lambda i, mt, sl: (i, 0)),
        ],
        out_specs=pl.BlockSpec(memory_space=pl.ANY),
        scratch_shapes=[pltpu.VMEM((RING, bm, d), F32),
                        pltpu.VMEM((2, f, d), BF16),
                        pltpu.SemaphoreType.DMA((RING,)), pltpu.SemaphoreType.DMA((2,))],
    )
    return pl.pallas_call(
        functools.partial(_down_kernel, nb=n_blocks),
        grid_spec=grid_spec,
        out_shape=jax.ShapeDtypeStruct((n_assign + bm, d), F32),
        compiler_params=_cparams(("arbitrary",)),
        name="down",
    )(meta, slot, h, wd_bf, wrow)


def _final_kernel(h_ref, y0_ref, y1_ref, nw_ref, o_ref):
    h = h_ref[...] + (y0_ref[...] + y1_ref[...])
    ms = jnp.mean(h * h, axis=-1, keepdims=True)
    o_ref[...] = h * lax.rsqrt(ms + NORM_EPS) * nw_ref[...]


def _final(h1, yc, norm_w, tm, row_off, n_tot):
    n, d = h1.shape
    nt = n // tm
    o0 = row_off // tm
    o1 = (n_tot + row_off) // tm
    return pl.pallas_call(
        _final_kernel,
        grid=(nt,),
        in_specs=[
            pl.BlockSpec((tm, d), lambda i: (i, 0)),
            pl.BlockSpec((tm, d), lambda i: (i + o0, 0)),
            pl.BlockSpec((tm, d), lambda i: (i + o1, 0)),
            pl.BlockSpec((1, d), lambda i: (0, 0)),
        ],
        out_specs=pl.BlockSpec((tm, d), lambda i: (i, 0)),
        out_shape=jax.ShapeDtypeStruct((n, d), F32),
        compiler_params=_cparams(("arbitrary",)),
        name="final",
    )(h1, yc, yc, norm_w.reshape(1, d))


def _tiles(seq, d):
    tm = min(512, seq)
    return dict(tm=tm, tn=min(512, d // 2), tq=min(256, seq), tm_out=tm,
                tn_out=min(1024, d), tm_fin=min(256, seq), bm=min(256, seq))


def _mixer(x, shared, prm, cast_ws):
    b, seq, d = x.shape
    n = b * seq
    t = _tiles(seq, d)
    x2d = x.reshape(n, d)
    qk, v, p = _inproj(x2d, prm["attn_norm_w"], shared["w_in"], shared["rope_real"], seq,
                       t["tm"], t["tn"])
    attn, cast = _attention(shared["lam"], qk, v, shared["qk_meta"], shared["v_meta"],
                            prm["subln_w"], b, seq, t["tq"], cast_ws)
    pool = _pool(p, shared["p_meta"], shared["pool_w"], prm["pool_scale"], seq, t["tm"])
    return _outproj(attn, pool, shared["w_out"], x2d, t["tm_out"], t["tn_out"]), cast


def _moe_and_final(h1a, h1b, shapes, shared, prm):
    seq, d = shapes[0][1], shapes[0][2]
    t = _tiles(seq, d)
    bm = t["bm"]
    n_a, n_b = h1a.shape[0], h1b.shape[0]
    n = n_a + n_b
    hn2, rt = _router(h1a, h1b, prm["ffn_norm_w"], shared["w_router"], t["tm"])
    meta, tok, slot, wrow, n_blocks = _route(rt, bm)
    h = _gateup(meta, tok, hn2, shared["w_gate"], shared["w_up"], n_blocks, bm)
    yc = _down(meta, slot, h, shared["w_down"], wrow, n * TOP_K, n_blocks, bm)
    out_a = _final(h1a, yc, prm["final_norm_w"], t["tm_fin"], 0, n)
    out_b = _final(h1b, yc, prm["final_norm_w"], t["tm_fin"], n_a, n)
    return out_a.reshape(shapes[0]), out_b.reshape(shapes[1])


def kernel(x_prompt, x_sample, meta_tokens, attn_norm_w, w_in, lambda_q1, lambda_k1, lambda_q2, lambda_k2, subln_w, pool_w, pool_scale, w_out, ffn_norm_w, router_group, router_expert, expert_w_gate, expert_w_up, expert_w_down, final_norm_w):
    d = x_prompt.shape[-1]
    seq = x_prompt.shape[1]
    assert x_sample.shape[1] == seq and attn_norm_w.shape[0] == 1
    prm = dict(attn_norm_w=attn_norm_w[0], subln_w=subln_w[0], pool_scale=pool_scale[0],
               ffn_norm_w=ffn_norm_w[0], final_norm_w=final_norm_w)
    wr = jnp.concatenate([router_group[0], router_expert[0]], axis=1)
    wr = jnp.pad(wr, ((0, 0), (0, LANES - wr.shape[1])))
    t = _tiles(seq, d)
    shared = dict(
        w_in=_prep_w_in(w_in[0], t["tn"]), w_out=w_out[0].astype(BF16), pool_w=pool_w[0].astype(BF16),
        w_router=wr.astype(BF16),
        lam=_lam(lambda_q1[0], lambda_k1[0], lambda_q2[0], lambda_k2[0]),
        rope_real=_rope_tables(jnp.arange(N_META, N_META + seq, dtype=F32)),
    )
    meta_pad = jnp.pad(meta_tokens.astype(F32), ((0, META_ROWS - N_META), (0, 0)))
    qk_m, v_m, p_m = _inproj(meta_pad, prm["attn_norm_w"], shared["w_in"],
                             _rope_tables(jnp.arange(META_ROWS, dtype=F32)), META_ROWS,
                             META_ROWS, t["tn"])
    shared.update(qk_meta=qk_m, v_meta=v_m, p_meta=p_m)
    h1a, (w_down_bf,) = _mixer(x_prompt, shared, prm, (expert_w_down[0],))
    h1b, (w_gate_bf, w_up_bf) = _mixer(x_sample, shared, prm,
                                       (expert_w_gate[0], expert_w_up[0]))
    shared.update(w_gate=w_gate_bf, w_up=w_up_bf, w_down=w_down_bf)
    return _moe_and_final(h1a, h1b, (x_prompt.shape, x_sample.shape), shared, prm)
```
